```python
import math
import jax
import jax.numpy as jnp
from jax import lax
import numpy as np

D_MODEL = 1024
BATCH = 8
SEQ = 4096
DEPTH = 1

HEAD_DIM = 64
SSD_HEADS = 16
SSD_GROUPS = 2
SSD_STATE = 128
SSD_CONV = 4
SSD_CHUNK = 128
SSD_WIDTH = SSD_HEADS * HEAD_DIM
SSD_BC_WIDTH = SSD_GROUPS * SSD_STATE
SSD_XBC_WIDTH = SSD_WIDTH + 2 * SSD_BC_WIDTH
FOX_HEADS = 16
FOX_WIDTH = FOX_HEADS * HEAD_DIM
Q_BLOCK = 128
MIX_WIDTH = SSD_WIDTH + FOX_WIDTH
D_FF = 2816
FFN_CONV = 3
NORM_EPS = 1e-6
Z_END = SSD_WIDTH
XBC_END = Z_END + SSD_XBC_WIDTH
DT_END = XBC_END + SSD_HEADS
Q_END = DT_END + FOX_WIDTH
K_END = Q_END + FOX_WIDTH
V_END = K_END + FOX_WIDTH
IN_COLS = V_END + FOX_HEADS

kernel_name = 'hybrid_ssd_fox_convffn_layer'


def rms_norm(x, w):
    xf = x.astype(jnp.float32)
    y = xf * lax.rsqrt(jnp.mean(xf * xf, axis=-1, keepdims=True) + NORM_EPS)
    return (y * w.astype(jnp.float32)).astype(x.dtype)


def causal_depthwise_conv(x, w, b):
    k = w.shape[0]
    y = lax.conv_general_dilated(
        x, w[:, None, :].astype(x.dtype), window_strides=(1,), padding=[(k - 1, 0)],
        dimension_numbers=('NWC', 'WIO', 'NWC'), feature_group_count=x.shape[-1])
    return y + b.astype(x.dtype)


def ssd_mixer(xbc_raw, z, dt_raw, conv_w, conv_b, dt_bias, a_log, d_skip, norm_w):
    b, s, _ = xbc_raw.shape
    nc = s // SSD_CHUNK
    e = SSD_HEADS // SSD_GROUPS
    xbc = jax.nn.silu(causal_depthwise_conv(xbc_raw, conv_w, conv_b))
    xs, bm, cm = jnp.split(xbc, [SSD_WIDTH, SSD_WIDTH + SSD_BC_WIDTH], axis=-1)
    x6 = xs.reshape(b, nc, SSD_CHUNK, SSD_GROUPS, e, HEAD_DIM)
    bm = bm.reshape(b, nc, SSD_CHUNK, SSD_GROUPS, SSD_STATE)
    cm = cm.reshape(b, nc, SSD_CHUNK, SSD_GROUPS, SSD_STATE)
    dt = jax.nn.softplus(dt_raw.astype(jnp.float32) + dt_bias.astype(jnp.float32))
    a = -jnp.exp(a_log.astype(jnp.float32))
    dt5 = dt.reshape(b, nc, SSD_CHUNK, SSD_GROUPS, e)
    a_cs = jnp.cumsum(dt5 * a.reshape(SSD_GROUPS, e), axis=2)
    xdt = x6 * dt5[..., None]
    at = jnp.moveaxis(a_cs, 2, -1)
    seg = at[..., :, None] - at[..., None, :]
    tri = jnp.tril(jnp.ones((SSD_CHUNK, SSD_CHUNK), dtype=bool))
    decay = jnp.exp(jnp.where(tri, seg, -jnp.inf))
    cb = jnp.einsum('bclgn,bcsgn->bcgls', cm, bm)
    scores = cb[:, :, :, None] * decay
    y_diag = jnp.einsum('bcgels,bcsgep->bclgep', scores, xdt)
    decay_to_end = jnp.exp(a_cs[:, :, -1:] - a_cs)
    chunk_states = jnp.einsum('bclgn,bclgep->bcgepn', bm, xdt * decay_to_end[..., None])
    chunk_decay = jnp.exp(a_cs[:, :, -1])

    def step(h, inp):
        dec, st = inp
        return h * dec[..., None, None] + st, h

    h0 = jnp.zeros((b, SSD_GROUPS, e, HEAD_DIM, SSD_STATE), jnp.float32)
    _, states_in = lax.scan(step, h0, (jnp.moveaxis(chunk_decay, 1, 0), jnp.moveaxis(chunk_states, 1, 0)))
    states_in = jnp.moveaxis(states_in, 0, 1)
    y_off = jnp.einsum('bclgn,bcgepn->bclgep', cm, states_in) * jnp.exp(a_cs)[..., None]
    y = y_diag + y_off + d_skip.astype(jnp.float32).reshape(SSD_GROUPS, e)[:, :, None] * x6
    y = y.reshape(b, s, SSD_WIDTH)
    yg = (y * jax.nn.silu(z.astype(jnp.float32))).reshape(b, s, SSD_GROUPS, SSD_WIDTH // SSD_GROUPS)
    yg = yg * lax.rsqrt(jnp.mean(yg * yg, axis=-1, keepdims=True) + NORM_EPS)
    return (yg.reshape(b, s, SSD_WIDTH) * norm_w.astype(jnp.float32)).astype(xbc_raw.dtype)


def fox_mixer(q, k, v, f_raw, f_bias, q_norm_w, k_norm_w):
    b, s, _ = q.shape

    def heads(t):
        return t.reshape(b, s, FOX_HEADS, HEAD_DIM).transpose(0, 2, 1, 3)

    qh = rms_norm(heads(q), q_norm_w)
    kh = rms_norm(heads(k), k_norm_w)
    vh = heads(v)
    log_f = jax.nn.log_sigmoid(f_raw.astype(jnp.float32) + f_bias.astype(jnp.float32))
    cum = jnp.cumsum(log_f, axis=1).transpose(0, 2, 1)
    scale = HEAD_DIM ** -0.5
    outs = []
    for start in range(0, s, Q_BLOCK):
        end = start + Q_BLOCK
        logits = jnp.einsum('bhqd,bhkd->bhqk', qh[:, :, start:end], kh[:, :, :end],
                            preferred_element_type=jnp.float32) * scale
        logits = logits + cum[:, :, start:end, None] - cum[:, :, None, :end]
        causal = (start + jnp.arange(Q_BLOCK))[:, None] >= jnp.arange(end)[None, :]
        p = jax.nn.softmax(jnp.where(causal, logits, -jnp.inf), axis=-1)
        outs.append(jnp.einsum('bhqk,bhkd->bhqd', p.astype(vh.dtype), vh[:, :, :end]))
    o = jnp.concatenate(outs, axis=2)
    return o.transpose(0, 2, 1, 3).reshape(b, s, FOX_WIDTH)


def conv_gated_mlp(x, w_up, conv_w, conv_b, w_down):
    h = x @ w_up.astype(x.dtype)
    h = causal_depthwise_conv(h, conv_w, conv_b)
    gate, val = jnp.split(h, 2, axis=-1)
    return (jax.nn.silu(gate) * val) @ w_down.astype(x.dtype)


def setup_inputs(seed: int = 0) -> dict:
    key = jax.random.key(seed)
    ks = jax.random.split(key, 20)
    L = DEPTH

    def nrm(k, shape, scale):
        return jax.random.normal(k, shape, jnp.float32) * scale

    def gain(k, n):
        return 1.0 + 0.02 * jax.random.normal(k, (L, n), jnp.float32)

    dt = jnp.exp(jax.random.uniform(ks[5], (L, SSD_HEADS), jnp.float32, math.log(1e-3), math.log(1e-1)))
    dt_bias = dt + jnp.log(-jnp.expm1(-dt))
    return {
        'x': nrm(ks[0], (BATCH, SEQ, D_MODEL), 1.0),
        'norm_mix_w': gain(ks[1], D_MODEL),
        'w_in': nrm(ks[2], (L, D_MODEL, IN_COLS), D_MODEL ** -0.5),
        'ssd_conv_w': nrm(ks[3], (L, SSD_CONV, SSD_XBC_WIDTH), SSD_CONV ** -0.5),
        'ssd_conv_b': nrm(ks[4], (L, SSD_XBC_WIDTH), 0.02),
        'ssd_dt_bias': dt_bias,
        'ssd_a_log': jnp.log(jax.random.uniform(ks[6], (L, SSD_HEADS), jnp.float32, 1.0, 16.0)),
        'ssd_d': 1.0 + 0.1 * jax.random.normal(ks[7], (L, SSD_HEADS), jnp.float32),
        'ssd_norm_w': gain(ks[8], SSD_WIDTH),
        'fox_f_bias': jax.random.uniform(ks[9], (L, FOX_HEADS), jnp.float32, 2.0, 5.0),
        'fox_q_norm_w': gain(ks[10], HEAD_DIM),
        'fox_k_norm_w': gain(ks[11], HEAD_DIM),
        'w_out': nrm(ks[12], (L, MIX_WIDTH, D_MODEL), MIX_WIDTH ** -0.5),
        'norm_ffn_w': gain(ks[13], D_MODEL),
        'w_up': nrm(ks[14], (L, D_MODEL, 2 * D_FF), D_MODEL ** -0.5),
        'ffn_conv_w': nrm(ks[15], (L, FFN_CONV, 2 * D_FF), FFN_CONV ** -0.5),
        'ffn_conv_b': nrm(ks[16], (L, 2 * D_FF), 0.02),
        'w_down': nrm(ks[17], (L, D_FF, D_MODEL), D_FF ** -0.5),
    }


def reference(x, norm_mix_w, w_in, ssd_conv_w, ssd_conv_b, ssd_dt_bias, ssd_a_log, ssd_d,
              ssd_norm_w, fox_f_bias, fox_q_norm_w, fox_k_norm_w, w_out, norm_ffn_w,
              w_up, ffn_conv_w, ffn_conv_b, w_down):
    for layer in range(DEPTH):
        h = rms_norm(x, norm_mix_w[layer])
        proj = h @ w_in[layer].astype(h.dtype)
        z, xbc, dt_raw, q, k, v, f_raw = jnp.split(
            proj, [Z_END, XBC_END, DT_END, Q_END, K_END, V_END], axis=-1)
        y_ssd = ssd_mixer(xbc, z, dt_raw, ssd_conv_w[layer], ssd_conv_b[layer], ssd_dt_bias[layer],
                          ssd_a_log[layer], ssd_d[layer], ssd_norm_w[layer])
        y_fox = fox_mixer(q, k, v, f_raw, fox_f_bias[layer], fox_q_norm_w[layer], fox_k_norm_w[layer])
        mixed = jnp.concatenate([y_ssd, y_fox.astype(y_ssd.dtype)], axis=-1)
        x = x + (mixed @ w_out[layer].astype(mixed.dtype)).astype(x.dtype)
        hf = rms_norm(x, norm_ffn_w[layer])
        x = x + conv_gated_mlp(hf, w_up[layer], ffn_conv_w[layer], ffn_conv_b[layer], w_down[layer]).astype(x.dtype)
    return x
```

```python
import functools
import math

import jax
import jax.numpy as jnp
from jax import lax
from jax.experimental import pallas as pl
from jax.experimental.pallas import tpu as pltpu

F32 = jnp.float32
BF16 = jnp.bfloat16

D_MODEL = 1024
HEAD_DIM = 64
SSD_HEADS = 16
SSD_GROUPS = 2
SSD_STATE = 128
SSD_CONV = 4
SSD_CHUNK = 128
SSD_WIDTH = SSD_HEADS * HEAD_DIM
SSD_BC_WIDTH = SSD_GROUPS * SSD_STATE
SSD_XBC_WIDTH = SSD_WIDTH + 2 * SSD_BC_WIDTH
FOX_HEADS = 16
FOX_WIDTH = FOX_HEADS * HEAD_DIM
D_FF = 2816
FFN_CONV = 3
NORM_EPS = 1e-6
LOG2E = math.log2(math.e)

LANES = 128
SUBLANES = 8
MXU_DIM = 256
PAIR = 2 * HEAD_DIM
NEG_BIG = -1e30

_Z0 = 0
_XBC0 = _Z0 + SSD_WIDTH
_Q0 = _XBC0 + SSD_XBC_WIDTH
_K0 = _Q0 + FOX_WIDTH
_V0 = _K0 + FOX_WIDTH
_DT0 = _V0 + FOX_WIDTH
_F0 = _DT0 + LANES
_IN_COLS_PADDED = _F0 + LANES

_VMEM_LIMIT = 56 * 1024 * 1024


def _const_spec(shape):
    zeros = (0,) * len(shape)
    return pl.BlockSpec(shape, lambda *_: zeros, pipeline_mode=pl.Buffered(1))


def _inproj_body(x_ref, nw_ref, w_ref, g_ref, qw_ref, kw_ref,
                 z_ref, xbc_ref, q_ref, k_ref, v_ref, dt_ref, f_ref):
    x = x_ref[...]
    ms = jnp.mean(x * x, axis=-1, keepdims=True)
    h = (x * lax.rsqrt(ms + NORM_EPS) * nw_ref[...]).astype(BF16)

    def proj(c0, n):
        return jnp.dot(h, w_ref[:, c0:c0 + n], preferred_element_type=F32)

    ch = 512
    for c in range(0, SSD_WIDTH, ch):
        z_ref[:, c:c + ch] = proj(_Z0 + c, ch).astype(BF16)
    for c in range(0, SSD_XBC_WIDTH, ch):
        xbc_ref[:, c:c + ch] = proj(_XBC0 + c, ch).astype(BF16)
    for c in range(0, FOX_WIDTH, ch):
        v_ref[:, c:c + ch] = proj(_V0 + c, ch).astype(BF16)

    g = g_ref[...]
    for base, o_ref, hw_ref in ((_Q0, q_ref, qw_ref), (_K0, k_ref, kw_ref)):
        for c in range(0, FOX_WIDTH, MXU_DIM):
            t = proj(base + c, MXU_DIM)
            ss = jnp.dot((t * t).astype(BF16), g, preferred_element_type=F32)
            o_ref[:, c:c + MXU_DIM] = (
                t * lax.rsqrt(ss * (1.0 / HEAD_DIM) + NORM_EPS) * hw_ref[...]).astype(BF16)

    dt_ref[...] = proj(_DT0, LANES)
    f_ref[...] = proj(_F0, LANES)


def _in_projection(x2, norm_w, w_r, g, qw, kw, tm):
    t = x2.shape[0]
    row = lambda n: pl.BlockSpec((tm, n), lambda i: (i, 0))
    out_shape = (
        jax.ShapeDtypeStruct((t, SSD_WIDTH), BF16),
        jax.ShapeDtypeStruct((t, SSD_XBC_WIDTH), BF16),
        jax.ShapeDtypeStruct((t, FOX_WIDTH), BF16),
        jax.ShapeDtypeStruct((t, FOX_WIDTH), BF16),
        jax.ShapeDtypeStruct((t, FOX_WIDTH), BF16),
        jax.ShapeDtypeStruct((t, LANES), F32),
        jax.ShapeDtypeStruct((t, LANES), F32),
    )
    return pl.pallas_call(
        _inproj_body,
        grid=(t // tm,),
        in_specs=[row(D_MODEL), _const_spec((1, D_MODEL)), _const_spec(w_r.shape),
                  _const_spec(g.shape), _const_spec(qw.shape), _const_spec(kw.shape)],
        out_specs=(row(SSD_WIDTH), row(SSD_XBC_WIDTH), row(FOX_WIDTH), row(FOX_WIDTH),
                   row(FOX_WIDTH), row(LANES), row(LANES)),
        out_shape=out_shape,
        compiler_params=pltpu.CompilerParams(
            dimension_semantics=("arbitrary",), vmem_limit_bytes=_VMEM_LIMIT),
        name="in_projection",
    )(x2, norm_w, w_r, g, qw, kw)


def _softplus(x):
    return jnp.maximum(x, 0.0) + jnp.log1p(jnp.exp(-jnp.abs(x)))


def _split3(v):
    hi = v.astype(BF16)
    r1 = v - hi.astype(F32)
    mid = r1.astype(BF16)
    lo = (r1 - mid.astype(F32)).astype(BF16)
    return hi, mid, lo


def _ssd_body(xbc_ref, z_ref, dt_ref, f_ref, cw_ref, cb_ref, dtb_ref, alog_ref, dskip_ref,
              nw_ref, fb_ref,
              y_ref, cum_ref, cumt_ref,
              state_ref, xext_ref, cumcarry_ref, xdd_ref, yg_ref):
    L = SSD_CHUNK
    c = pl.program_id(1)

    @pl.when(c == 0)
    def _():
        state_ref[...] = jnp.zeros_like(state_ref)
        xext_ref[0:SUBLANES, :] = jnp.zeros((SUBLANES, SSD_XBC_WIDTH), F32)
        cumcarry_ref[...] = jnp.zeros_like(cumcarry_ref)

    xext_ref[SUBLANES:SUBLANES + L, :] = xbc_ref[0].astype(F32)
    conv = cb_ref[...] + cw_ref[3:4, :] * xext_ref[SUBLANES:SUBLANES + L, :]
    for k in range(SSD_CONV - 1):
        off = SUBLANES - (SSD_CONV - 1) + k
        conv = conv + cw_ref[k:k + 1, :] * xext_ref[off:off + L, :]
    xext_ref[0:SUBLANES, :] = xext_ref[L:L + SUBLANES, :]
    xbc = conv * (1.0 / (1.0 + jnp.exp(-conv)))

    xs = xbc[:, :SSD_WIDTH]
    bm = xbc[:, SSD_WIDTH:SSD_WIDTH + SSD_BC_WIDTH]
    cm = xbc[:, SSD_WIDTH + SSD_BC_WIDTH:]

    dt = _softplus(dt_ref[0] + dtb_ref[...])
    da = dt * (-jnp.exp(alog_ref[...]))
    logf = -_softplus(-(f_ref[0] + fb_ref[...]))

    rows = lax.broadcasted_iota(jnp.int32, (L, L), 0)
    cols = lax.broadcasted_iota(jnp.int32, (L, L), 1)
    tri = rows >= cols
    tri_b = jnp.where(tri, 1.0, 0.0).astype(BF16)
    parts = jnp.concatenate(_split3(da) + _split3(logf), axis=1)
    sums = jnp.dot(tri_b, parts, preferred_element_type=F32)
    a_cs = sums[:, 0:LANES] + sums[:, LANES:2 * LANES] + sums[:, 2 * LANES:3 * LANES]
    cum = (sums[:, 3 * LANES:4 * LANES] + sums[:, 4 * LANES:5 * LANES]
           + sums[:, 5 * LANES:6 * LANES]) + cumcarry_ref[0:1, :]
    cumcarry_ref[0:1, :] = cum[L - 1:L, :]
    cum_ref[0] = cum
    cumt_ref[0] = cum.T

    a_cst = a_cs.T
    ea = jnp.exp(a_cs)
    a_end = a_cs[L - 1:L, :]
    dte = jnp.exp(a_end - a_cs)
    cd = jnp.exp(a_end)

    lane = lax.broadcasted_iota(jnp.int32, (L, PAIR), 1)
    lo_half = lane < HEAD_DIM
    heads_per_group = SSD_HEADS // SSD_GROUPS
    pairs_per_group = heads_per_group // 2

    cbs, cms, bms = [], [], []
    for g in range(SSD_GROUPS):
        cm_g = cm[:, g * SSD_STATE:(g + 1) * SSD_STATE]
        bm_g = bm[:, g * SSD_STATE:(g + 1) * SSD_STATE]
        cbs.append(lax.dot_general(cm_g.astype(BF16), bm_g.astype(BF16),
                                   (((1,), (1,)), ((), ())), preferred_element_type=F32))
        cms.append(cm_g)
        bms.append(bm_g)

    ssq = [None] * SSD_GROUPS
    cd_parts = []
    for p in range(SSD_HEADS // 2):
        g = p // pairs_per_group
        h0, h1 = 2 * p, 2 * p + 1
        sl = slice(p * PAIR, (p + 1) * PAIR)
        xs_p = xs[:, sl]
        xdt_p = xs_p * jnp.where(lo_half, dt[:, h0:h0 + 1], dt[:, h1:h1 + 1])
        xdd_ref[:, sl] = (xdt_p * jnp.where(lo_half, dte[:, h0:h0 + 1], dte[:, h1:h1 + 1])).astype(BF16)
        cd_parts.append(jnp.where(lo_half[0:1, :], cd[:, h0:h0 + 1], cd[:, h1:h1 + 1]))
        rhs = jnp.concatenate([xdt_p.astype(BF16), state_ref[:, sl].astype(BF16)], axis=0)
        ys = []
        for h in (h0, h1):
            seg = a_cs[:, h:h + 1] - a_cst[h:h + 1, :]
            decay = jnp.exp(jnp.where(tri, seg, NEG_BIG))
            lhs = jnp.concatenate([(cbs[g] * decay).astype(BF16),
                                   (cms[g] * ea[:, h:h + 1]).astype(BF16)], axis=1)
            ys.append(jnp.dot(lhs, rhs, preferred_element_type=F32))
        y_p = jnp.where(lo_half, ys[0], ys[1]) + dskip_ref[:, sl] * xs_p
        zp = z_ref[0, :, sl].astype(F32)
        yg = y_p * (zp * (1.0 / (1.0 + jnp.exp(-zp))))
        yg_ref[:, sl] = yg
        s2 = jnp.sum(yg * yg, axis=-1, keepdims=True)
        ssq[g] = s2 if ssq[g] is None else ssq[g] + s2

    gw = SSD_WIDTH // SSD_GROUPS
    cd_row = jnp.concatenate(cd_parts, axis=1)
    for g in range(SSD_GROUPS):
        gs = slice(g * gw, (g + 1) * gw)
        scale = lax.rsqrt(ssq[g] * (1.0 / gw) + NORM_EPS)
        y_ref[0, :, gs] = (yg_ref[:, gs] * scale * nw_ref[:, gs]).astype(BF16)
        upd = jnp.dot(bms[g].T.astype(BF16), xdd_ref[:, gs], preferred_element_type=F32)
        state_ref[:, gs] = state_ref[:, gs] * cd_row[:, gs] + upd


def _ssd_mixer(xbc, z, dt_raw, f_raw, conv_w, conv_b, dt_bias_p, alog_p, dskip_e, norm_w, fbias_p):
    b, s, _ = xbc.shape
    L = SSD_CHUNK
    blk = lambda n: pl.BlockSpec((1, L, n), lambda i, j: (i, j, 0))
    return pl.pallas_call(
        _ssd_body,
        grid=(b, s // L),
        in_specs=[blk(SSD_XBC_WIDTH), blk(SSD_WIDTH), blk(LANES), blk(LANES),
                  _const_spec(conv_w.shape), _const_spec(conv_b.shape), _const_spec(dt_bias_p.shape),
                  _const_spec(alog_p.shape), _const_spec(dskip_e.shape), _const_spec(norm_w.shape),
                  _const_spec(fbias_p.shape)],
        out_specs=(blk(SSD_WIDTH), blk(LANES),
                   pl.BlockSpec((1, LANES, L), lambda i, j: (i, 0, j))),
        out_shape=(jax.ShapeDtypeStruct((b, s, SSD_WIDTH), BF16),
                   jax.ShapeDtypeStruct((b, s, LANES), F32),
                   jax.ShapeDtypeStruct((b, LANES, s), F32)),
        scratch_shapes=[
            pltpu.VMEM((SSD_STATE, SSD_WIDTH), F32),
            pltpu.VMEM((L + SUBLANES, SSD_XBC_WIDTH), F32),
            pltpu.VMEM((SUBLANES, LANES), F32),
            pltpu.VMEM((L, SSD_WIDTH), BF16),
            pltpu.VMEM((L, SSD_WIDTH), F32),
        ],
        compiler_params=pltpu.CompilerParams(
            dimension_semantics=("arbitrary", "arbitrary"), vmem_limit_bytes=_VMEM_LIMIT),
        name="ssd_mixer",
    )(xbc, z, dt_raw, f_raw, conv_w, conv_b, dt_bias_p, alog_p, dskip_e, norm_w, fbias_p)


def _fox_body(q_ref, k_ref, v_ref, cum_ref, cumt_ref, o_ref, m_ref, l_ref, acc_ref, *, blk):
    s = q_ref.shape[1]
    hp = pl.program_id(1)
    lane = lax.broadcasted_iota(jnp.int32, (blk, PAIR), 1)
    lo_half = lane < HEAD_DIM
    rows = lax.broadcasted_iota(jnp.int32, (blk, blk), 0)
    cols = lax.broadcasted_iota(jnp.int32, (blk, blk), 1)
    causal = rows >= cols

    def q_block(qi, _):
        q0 = pl.multiple_of(qi * blk, blk)
        q = q_ref[0, pl.ds(q0, blk), :]
        zero = jnp.zeros_like(q)
        qh = (jnp.where(lo_half, q, zero), jnp.where(lo_half, zero, q))
        cumblk = cum_ref[0, pl.ds(q0, blk), :]
        cq = tuple(
            jnp.sum(jnp.where(lane == 2 * hp + j, cumblk, 0.0), axis=-1, keepdims=True) * LOG2E
            for j in range(2))
        m_ref[...] = jnp.full(m_ref.shape, NEG_BIG, F32)
        l_ref[...] = jnp.zeros_like(l_ref)
        acc_ref[...] = jnp.zeros_like(acc_ref)

        def kv_step(j, masked):
            k0 = pl.multiple_of(j * blk, blk)
            k = k_ref[0, pl.ds(k0, blk), :]
            v = v_ref[0, pl.ds(k0, blk), :]
            ck = cumt_ref[0, 0, :, pl.ds(k0, blk)] * LOG2E
            for h in range(2):
                sc = lax.dot_general(qh[h], k, (((1,), (1,)), ((), ())),
                                     preferred_element_type=F32)
                sc = sc - ck[h:h + 1, :]
                if masked:
                    sc = jnp.where(causal, sc, NEG_BIG)
                m_prev = m_ref[h]
                m_new = jnp.maximum(m_prev, jnp.max(sc, axis=-1, keepdims=True) + cq[h])
                p = jnp.exp2(sc - (m_new - cq[h]))
                alpha = jnp.exp2(m_prev - m_new)
                l_ref[h] = alpha * l_ref[h] + jnp.sum(p, axis=-1, keepdims=True)
                acc_ref[h] = alpha * acc_ref[h] + jnp.dot(
                    p.astype(BF16), v, preferred_element_type=F32)
                m_ref[h] = m_new

        def off_diag(j, _):
            kv_step(j, False)
            return 0

        lax.fori_loop(0, qi, off_diag, 0)
        kv_step(qi, True)

        out0 = acc_ref[0] * (1.0 / l_ref[0])
        out1 = acc_ref[1] * (1.0 / l_ref[1])
        o_ref[0, pl.ds(q0, blk), :] = jnp.where(lo_half, out0, out1).astype(BF16)
        return 0

    lax.fori_loop(0, s // blk, q_block, 0)


def _fox_attention(q, k, v, cum, cumt_pairs, blk):
    b, s, _ = q.shape
    n_pairs = FOX_HEADS // 2
    pair_blk = pl.BlockSpec((1, s, PAIR), lambda i, j: (i, 0, j))
    return pl.pallas_call(
        functools.partial(_fox_body, blk=blk),
        grid=(b, n_pairs),
        in_specs=[pair_blk, pair_blk, pair_blk,
                  pl.BlockSpec((1, s, LANES), lambda i, j: (i, 0, 0)),
                  pl.BlockSpec((1, 1, 2, s), lambda i, j: (i, j, 0, 0))],
        out_specs=pair_blk,
        out_shape=jax.ShapeDtypeStruct((b, s, FOX_WIDTH), BF16),
        scratch_shapes=[
            pltpu.VMEM((2, blk, 1), F32),
            pltpu.VMEM((2, blk, 1), F32),
            pltpu.VMEM((2, blk, PAIR), F32),
        ],
        compiler_params=pltpu.CompilerParams(
            dimension_semantics=("arbitrary", "arbitrary"), vmem_limit_bytes=_VMEM_LIMIT),
        name="fox_attention",
    )(q, k, v, cum, cumt_pairs)


def _ffn_body(x_ref, ys_ref, yf_ref, wo_ref, nw_ref, wup_ref, cw_ref, cb_ref, wdn_ref,
              o_ref, carry_ref, hg_ref, hv_ref, act_ref, *, tm, ch):
    @pl.when(pl.program_id(1) == 0)
    def _():
        carry_ref[...] = jnp.zeros_like(carry_ref)

    x1 = (x_ref[0]
          + jnp.dot(ys_ref[0], wo_ref[0:SSD_WIDTH, :], preferred_element_type=F32)
          + jnp.dot(yf_ref[0], wo_ref[SSD_WIDTH:, :], preferred_element_type=F32))
    o_ref[0] = x1
    ms = jnp.mean(x1 * x1, axis=-1, keepdims=True)
    hf = (x1 * lax.rsqrt(ms + NORM_EPS) * nw_ref[...]).astype(BF16)

    hist = SUBLANES

    def conv_half(h_ref, c0):
        cs = slice(c0, c0 + ch)
        h_ref[0:hist, :] = carry_ref[:, cs]
        h_ref[hist:hist + tm, :] = jnp.dot(hf, wup_ref[:, cs], preferred_element_type=F32)
        carry_ref[:, cs] = h_ref[tm:tm + hist, :]
        out = cb_ref[:, cs] + cw_ref[FFN_CONV - 1:FFN_CONV, cs] * h_ref[hist:hist + tm, :]
        for k in range(FFN_CONV - 1):
            off = hist - (FFN_CONV - 1) + k
            out = out + cw_ref[k:k + 1, cs] * h_ref[off:off + tm, :]
        return out

    for c0 in range(0, D_FF, ch):
        gate = conv_half(hg_ref, c0)
        val = conv_half(hv_ref, D_FF + c0)
        act_ref[:, c0:c0 + ch] = (gate * (1.0 / (1.0 + jnp.exp(-gate))) * val).astype(BF16)

    o_ref[0] = o_ref[0] + jnp.dot(act_ref[...], wdn_ref[...], preferred_element_type=F32)


def _out_ffn(x, y_ssd, y_fox, w_out, norm_w, w_up, conv_w, conv_b, w_down, tm, ch):
    b, s, _ = x.shape
    blk = lambda n: pl.BlockSpec((1, tm, n), lambda i, j: (i, j, 0))
    return pl.pallas_call(
        functools.partial(_ffn_body, tm=tm, ch=ch),
        grid=(b, s // tm),
        in_specs=[blk(D_MODEL), blk(SSD_WIDTH), blk(FOX_WIDTH),
                  _const_spec(w_out.shape), _const_spec(norm_w.shape), _const_spec(w_up.shape),
                  _const_spec(conv_w.shape), _const_spec(conv_b.shape), _const_spec(w_down.shape)],
        out_specs=blk(D_MODEL),
        out_shape=jax.ShapeDtypeStruct((b, s, D_MODEL), F32),
        scratch_shapes=[
            pltpu.VMEM((SUBLANES, 2 * D_FF), F32),
            pltpu.VMEM((tm + SUBLANES, ch), F32),
            pltpu.VMEM((tm + SUBLANES, ch), F32),
            pltpu.VMEM((tm, D_FF), BF16),
        ],
        compiler_params=pltpu.CompilerParams(
            dimension_semantics=("arbitrary", "arbitrary"), vmem_limit_bytes=_VMEM_LIMIT),
        name="out_ffn",
    )(x, y_ssd, y_fox, w_out, norm_w, w_up, conv_w, conv_b, w_down)


def _pad_lanes(v):
    return jnp.pad(v.astype(F32), (0, LANES - v.shape[0]))[None, :]


def _layer(x, norm_mix_w, w_in, ssd_conv_w, ssd_conv_b, ssd_dt_bias, ssd_a_log, ssd_d, ssd_norm_w,
           fox_f_bias, fox_q_norm_w, fox_k_norm_w, w_out, norm_ffn_w, w_up, ffn_conv_w, ffn_conv_b,
           w_down, *, tm_in, attn_blk, tm_ffn, ffn_ch):
    b, s, d = x.shape
    z_end = SSD_WIDTH
    xbc_end = z_end + SSD_XBC_WIDTH
    dt_end = xbc_end + SSD_HEADS
    q_end = dt_end + FOX_WIDTH
    k_end = q_end + FOX_WIDTH
    v_end = k_end + FOX_WIDTH

    def pad_cols(w):
        return jnp.pad(w, ((0, 0), (0, LANES - w.shape[1])))

    w_r = jnp.concatenate(
        [w_in[:, :xbc_end], w_in[:, dt_end:v_end],
         pad_cols(w_in[:, xbc_end:dt_end]), pad_cols(w_in[:, v_end:])], axis=1).astype(BF16)
    heads_per_tile = MXU_DIM // HEAD_DIM
    g = jnp.kron(jnp.eye(heads_per_tile, dtype=F32), jnp.ones((HEAD_DIM, HEAD_DIM), F32)).astype(BF16)
    qw = (jnp.tile(fox_q_norm_w.astype(F32), heads_per_tile) * (HEAD_DIM ** -0.5 * LOG2E))[None, :]
    kw = jnp.tile(fox_k_norm_w.astype(F32), heads_per_tile)[None, :]

    z, xbc, q, k, v, dt_raw, f_raw = _in_projection(
        x.reshape(b * s, d), norm_mix_w[None, :].astype(F32), w_r, g, qw, kw, tm_in)
    r3 = lambda a: a.reshape(b, s, a.shape[-1])

    y_ssd, cum, cumt = _ssd_mixer(
        r3(xbc), r3(z), r3(dt_raw), r3(f_raw),
        ssd_conv_w.astype(F32), ssd_conv_b[None, :].astype(F32),
        _pad_lanes(ssd_dt_bias), _pad_lanes(ssd_a_log),
        jnp.repeat(ssd_d.astype(F32), HEAD_DIM)[None, :], ssd_norm_w[None, :].astype(F32),
        _pad_lanes(fox_f_bias))

    cumt_pairs = cumt[:, :FOX_HEADS, :].reshape(b, FOX_HEADS // 2, 2, s)
    y_fox = _fox_attention(r3(q), r3(k), r3(v), cum, cumt_pairs, attn_blk)

    return _out_ffn(x, y_ssd, y_fox, w_out.astype(BF16), norm_ffn_w[None, :].astype(F32),
                    w_up.astype(BF16), ffn_conv_w.astype(F32), ffn_conv_b[None, :].astype(F32),
                    w_down.astype(BF16), tm_ffn, ffn_ch)


def kernel(x, norm_mix_w, w_in, ssd_conv_w, ssd_conv_b, ssd_dt_bias, ssd_a_log, ssd_d, ssd_norm_w,
           fox_f_bias, fox_q_norm_w, fox_k_norm_w, w_out, norm_ffn_w, w_up, ffn_conv_w, ffn_conv_b,
           w_down):
    depth = w_in.shape[0]
    for layer in range(depth):
        x = _layer(x, norm_mix_w[layer], w_in[layer], ssd_conv_w[layer], ssd_conv_b[layer],
                   ssd_dt_bias[layer], ssd_a_log[layer], ssd_d[layer], ssd_norm_w[layer],
                   fox_f_bias[layer], fox_q_norm_w[layer], fox_k_norm_w[layer], w_out[layer],
                   norm_ffn_w[layer], w_up[layer], ffn_conv_w[layer], ffn_conv_b[layer],
                   w_down[layer], tm_in=512, attn_blk=256, tm_ffn=512, ffn_ch=256)
    return x
```

```python
import functools
import math

import jax
import jax.numpy as jnp
from jax import lax
from jax.experimental import pallas as pl
from jax.experimental.pallas import tpu as pltpu

F32 = jnp.float32
BF16 = jnp.bfloat16

D_MODEL = 1024
HEAD_DIM = 64
SSD_HEADS = 16
SSD_GROUPS = 2
SSD_STATE = 128
SSD_CONV = 4
SSD_CHUNK = 128
SSD_WIDTH = SSD_HEADS * HEAD_DIM
SSD_BC_WIDTH = SSD_GROUPS * SSD_STATE
SSD_XBC_WIDTH = SSD_WIDTH + 2 * SSD_BC_WIDTH
FOX_HEADS = 16
FOX_WIDTH = FOX_HEADS * HEAD_DIM
D_FF = 2816
FFN_CONV = 3
NORM_EPS = 1e-6
LOG2E = math.log2(math.e)

LANES = 128
SUBLANES = 8
MXU_DIM = 256
PAIR = 2 * HEAD_DIM
NEG_BIG = -1e30

_Z0 = 0
_XBC0 = _Z0 + SSD_WIDTH
_Q0 = _XBC0 + SSD_XBC_WIDTH
_K0 = _Q0 + FOX_WIDTH
_V0 = _K0 + FOX_WIDTH
_DT0 = _V0 + FOX_WIDTH
_F0 = _DT0 + LANES
_IN_COLS_PADDED = _F0 + LANES

_VMEM_LIMIT = 56 * 1024 * 1024


def _const_spec(shape):
    zeros = (0,) * len(shape)
    return pl.BlockSpec(shape, lambda *_: zeros, pipeline_mode=pl.Buffered(1))


def _inproj_body(x_ref, nw_ref, w_ref, g_ref, qw_ref, kw_ref,
                 z_ref, xbc_ref, q_ref, k_ref, vt_ref, dt_ref, f_ref, vtmp_ref):
    x = x_ref[...]
    ms = jnp.mean(x * x, axis=-1, keepdims=True)
    h = (x * lax.rsqrt(ms + NORM_EPS) * nw_ref[...]).astype(BF16)

    def proj(c0, n):
        return jnp.dot(h, w_ref[:, c0:c0 + n], preferred_element_type=F32)

    ch = 512
    for c in range(0, SSD_WIDTH, ch):
        z_ref[:, c:c + ch] = proj(_Z0 + c, ch).astype(BF16)
    for c in range(0, SSD_XBC_WIDTH, ch):
        xbc_ref[:, c:c + ch] = proj(_XBC0 + c, ch).astype(BF16)
    for c in range(0, FOX_WIDTH, MXU_DIM):
        vtmp_ref[...] = proj(_V0 + c, MXU_DIM)
        vt_ref[c:c + MXU_DIM, :] = vtmp_ref[...].T.astype(BF16)

    g = g_ref[...]
    for base, o_ref, hw_ref in ((_Q0, q_ref, qw_ref), (_K0, k_ref, kw_ref)):
        for c in range(0, FOX_WIDTH, MXU_DIM):
            t = proj(base + c, MXU_DIM)
            ss = jnp.dot((t * t).astype(BF16), g, preferred_element_type=F32)
            o_ref[:, c:c + MXU_DIM] = (
                t * lax.rsqrt(ss * (1.0 / HEAD_DIM) + NORM_EPS) * hw_ref[...]).astype(BF16)

    dt_ref[...] = proj(_DT0, LANES)
    f_ref[...] = proj(_F0, LANES)


def _in_projection(x2, norm_w, w_r, g, qw, kw, tm):
    t = x2.shape[0]
    row = lambda n: pl.BlockSpec((tm, n), lambda i: (i, 0))
    out_shape = (
        jax.ShapeDtypeStruct((t, SSD_WIDTH), BF16),
        jax.ShapeDtypeStruct((t, SSD_XBC_WIDTH), BF16),
        jax.ShapeDtypeStruct((t, FOX_WIDTH), BF16),
        jax.ShapeDtypeStruct((t, FOX_WIDTH), BF16),
        jax.ShapeDtypeStruct((FOX_WIDTH, t), BF16),
        jax.ShapeDtypeStruct((t, LANES), F32),
        jax.ShapeDtypeStruct((t, LANES), F32),
    )
    return pl.pallas_call(
        _inproj_body,
        grid=(t // tm,),
        in_specs=[row(D_MODEL), _const_spec((1, D_MODEL)), _const_spec(w_r.shape),
                  _const_spec(g.shape), _const_spec(qw.shape), _const_spec(kw.shape)],
        out_specs=(row(SSD_WIDTH), row(SSD_XBC_WIDTH), row(FOX_WIDTH), row(FOX_WIDTH),
                   pl.BlockSpec((FOX_WIDTH, tm), lambda i: (0, i)), row(LANES), row(LANES)),
        out_shape=out_shape,
        scratch_shapes=[pltpu.VMEM((tm, MXU_DIM), F32)],
        compiler_params=pltpu.CompilerParams(
            dimension_semantics=("arbitrary",), vmem_limit_bytes=_VMEM_LIMIT),
        name="in_projection",
    )(x2, norm_w, w_r, g, qw, kw)


def _softplus(x):
    return jnp.maximum(x, 0.0) + jnp.log1p(jnp.exp(-jnp.abs(x)))


def _split3(v):
    hi = v.astype(BF16)
    r1 = v - hi.astype(F32)
    mid = r1.astype(BF16)
    lo = (r1 - mid.astype(F32)).astype(BF16)
    return hi, mid, lo


def _ssd_body(xbc_ref, z_ref, dt_ref, f_ref, cw_ref, cb_ref, dtb_ref, alog_ref, dskip_ref,
              nw_ref, fb_ref, sel_ref,
              y_ref, kbias_ref, cumt_ref,
              state_ref, xext_ref, cumcarry_ref, xdd_ref, yg_ref):
    L = SSD_CHUNK
    c = pl.program_id(1)

    @pl.when(c == 0)
    def _():
        state_ref[...] = jnp.zeros_like(state_ref)
        xext_ref[0:SUBLANES, :] = jnp.zeros((SUBLANES, SSD_XBC_WIDTH), F32)
        cumcarry_ref[...] = jnp.zeros_like(cumcarry_ref)

    xext_ref[SUBLANES:SUBLANES + L, :] = xbc_ref[0].astype(F32)
    conv = cb_ref[...] + cw_ref[3:4, :] * xext_ref[SUBLANES:SUBLANES + L, :]
    for k in range(SSD_CONV - 1):
        off = SUBLANES - (SSD_CONV - 1) + k
        conv = conv + cw_ref[k:k + 1, :] * xext_ref[off:off + L, :]
    xext_ref[0:SUBLANES, :] = xext_ref[L:L + SUBLANES, :]
    xbc = conv * (1.0 / (1.0 + jnp.exp(-conv)))

    xs = xbc[:, :SSD_WIDTH]
    bm = xbc[:, SSD_WIDTH:SSD_WIDTH + SSD_BC_WIDTH]
    cm = xbc[:, SSD_WIDTH + SSD_BC_WIDTH:]

    dt = _softplus(dt_ref[0] + dtb_ref[...])
    da = dt * (-jnp.exp(alog_ref[...]))
    logf = -_softplus(-(f_ref[0] + fb_ref[...]))

    rows = lax.broadcasted_iota(jnp.int32, (L, L), 0)
    cols = lax.broadcasted_iota(jnp.int32, (L, L), 1)
    tri = rows >= cols
    tri_b = jnp.where(tri, 1.0, 0.0).astype(BF16)
    parts = jnp.concatenate(_split3(da) + _split3(logf), axis=1)
    sums = jnp.dot(tri_b, parts, preferred_element_type=F32)
    a_cs = sums[:, 0:LANES] + sums[:, LANES:2 * LANES] + sums[:, 2 * LANES:3 * LANES]
    cum = (sums[:, 3 * LANES:4 * LANES] + sums[:, 4 * LANES:5 * LANES]
           + sums[:, 5 * LANES:6 * LANES]) + cumcarry_ref[0:1, :]
    cumcarry_ref[0:1, :] = cum[L - 1:L, :]
    cumt_ref[0] = cum.T
    kb = jnp.concatenate(_split3(cum * (-LOG2E)), axis=1)
    kbias_ref[0] = jnp.dot(kb, sel_ref[...], preferred_element_type=F32).astype(BF16)

    a_cst = a_cs.T
    ea = jnp.exp(a_cs)
    a_end = a_cs[L - 1:L, :]
    dte = jnp.exp(a_end - a_cs)
    cd = jnp.exp(a_end)

    lane = lax.broadcasted_iota(jnp.int32, (L, PAIR), 1)
    lo_half = lane < HEAD_DIM
    heads_per_group = SSD_HEADS // SSD_GROUPS
    pairs_per_group = heads_per_group // 2

    cbs, cms, bms = [], [], []
    for g in range(SSD_GROUPS):
        cm_g = cm[:, g * SSD_STATE:(g + 1) * SSD_STATE]
        bm_g = bm[:, g * SSD_STATE:(g + 1) * SSD_STATE]
        cbs.append(lax.dot_general(cm_g.astype(BF16), bm_g.astype(BF16),
                                   (((1,), (1,)), ((), ())), preferred_element_type=F32))
        cms.append(cm_g)
        bms.append(bm_g)

    ssq = [None] * SSD_GROUPS
    cd_parts = []
    for p in range(SSD_HEADS // 2):
        g = p // pairs_per_group
        h0, h1 = 2 * p, 2 * p + 1
        sl = slice(p * PAIR, (p + 1) * PAIR)
        xs_p = xs[:, sl]
        xdt_p = xs_p * jnp.where(lo_half, dt[:, h0:h0 + 1], dt[:, h1:h1 + 1])
        xdd_ref[:, sl] = (xdt_p * jnp.where(lo_half, dte[:, h0:h0 + 1], dte[:, h1:h1 + 1])).astype(BF16)
        cd_parts.append(jnp.where(lo_half[0:1, :], cd[:, h0:h0 + 1], cd[:, h1:h1 + 1]))
        rhs = jnp.concatenate([xdt_p.astype(BF16), state_ref[:, sl].astype(BF16)], axis=0)
        ys = []
        for h in (h0, h1):
            seg = a_cs[:, h:h + 1] - a_cst[h:h + 1, :]
            decay = jnp.exp(jnp.where(tri, seg, NEG_BIG))
            lhs = jnp.concatenate([(cbs[g] * decay).astype(BF16),
                                   (cms[g] * ea[:, h:h + 1]).astype(BF16)], axis=1)
            ys.append(jnp.dot(lhs, rhs, preferred_element_type=F32))
        y_p = jnp.where(lo_half, ys[0], ys[1]) + dskip_ref[:, sl] * xs_p
        zp = z_ref[0, :, sl].astype(F32)
        yg = y_p * (zp * (1.0 / (1.0 + jnp.exp(-zp))))
        yg_ref[:, sl] = yg
        s2 = jnp.sum(yg * yg, axis=-1, keepdims=True)
        ssq[g] = s2 if ssq[g] is None else ssq[g] + s2

    gw = SSD_WIDTH // SSD_GROUPS
    cd_row = jnp.concatenate(cd_parts, axis=1)
    for g in range(SSD_GROUPS):
        gs = slice(g * gw, (g + 1) * gw)
        scale = lax.rsqrt(ssq[g] * (1.0 / gw) + NORM_EPS)
        y_ref[0, :, gs] = (yg_ref[:, gs] * scale * nw_ref[:, gs]).astype(BF16)
        upd = jnp.dot(bms[g].T.astype(BF16), xdd_ref[:, gs], preferred_element_type=F32)
        state_ref[:, gs] = state_ref[:, gs] * cd_row[:, gs] + upd


_KBIAS_TERMS = 3


def _kbias_selector():
    import numpy as np
    sel = np.zeros((_KBIAS_TERMS * LANES, FOX_WIDTH), np.float32)
    for h in range(FOX_HEADS):
        base = (h // 2) * PAIR + (HEAD_DIM if h % 2 == 0 else 0)
        for i in range(_KBIAS_TERMS):
            sel[i * LANES + h, base + i] = 1.0
    return jnp.asarray(sel, BF16)


def _ssd_mixer(xbc, z, dt_raw, f_raw, conv_w, conv_b, dt_bias_p, alog_p, dskip_e, norm_w, fbias_p):
    b, s, _ = xbc.shape
    L = SSD_CHUNK
    sel = _kbias_selector()
    blk = lambda n: pl.BlockSpec((1, L, n), lambda i, j: (i, j, 0))
    return pl.pallas_call(
        _ssd_body,
        grid=(b, s // L),
        in_specs=[blk(SSD_XBC_WIDTH), blk(SSD_WIDTH), blk(LANES), blk(LANES),
                  _const_spec(conv_w.shape), _const_spec(conv_b.shape), _const_spec(dt_bias_p.shape),
                  _const_spec(alog_p.shape), _const_spec(dskip_e.shape), _const_spec(norm_w.shape),
                  _const_spec(fbias_p.shape), _const_spec(sel.shape)],
        out_specs=(blk(SSD_WIDTH), blk(FOX_WIDTH),
                   pl.BlockSpec((1, LANES, L), lambda i, j: (i, 0, j))),
        out_shape=(jax.ShapeDtypeStruct((b, s, SSD_WIDTH), BF16),
                   jax.ShapeDtypeStruct((b, s, FOX_WIDTH), BF16),
                   jax.ShapeDtypeStruct((b, LANES, s), F32)),
        scratch_shapes=[
            pltpu.VMEM((SSD_STATE, SSD_WIDTH), F32),
            pltpu.VMEM((L + SUBLANES, SSD_XBC_WIDTH), F32),
            pltpu.VMEM((SUBLANES, LANES), F32),
            pltpu.VMEM((L, SSD_WIDTH), BF16),
            pltpu.VMEM((L, SSD_WIDTH), F32),
        ],
        compiler_params=pltpu.CompilerParams(
            dimension_semantics=("arbitrary", "arbitrary"), vmem_limit_bytes=_VMEM_LIMIT),
        name="ssd_mixer",
    )(xbc, z, dt_raw, f_raw, conv_w, conv_b, dt_bias_p, alog_p, dskip_e, norm_w, fbias_p, sel)


def _fox_body(q_ref, k_ref, vt_ref, kbias_ref, cumt_ref, o_ref,
              kaug_ref, m_ref, l_ref, acc_ref, *, blk):
    s = q_ref.shape[1]
    nblk = s // blk
    lane = lax.broadcasted_iota(jnp.int32, (blk, PAIR), 1).astype(F32).astype(BF16)
    lo_half = lane < HEAD_DIM

    def build_keys(i, _):
        r0 = pl.multiple_of(i * blk, blk)
        kp = k_ref[0, pl.ds(r0, blk), :]
        kb = kbias_ref[0, pl.ds(r0, blk), :]
        kaug_ref[0, pl.ds(r0, blk), :] = jnp.where(lo_half, kp, kb)
        kaug_ref[1, pl.ds(r0, blk), :] = jnp.where(lo_half, kb, kp)
        return 0

    lax.fori_loop(0, nblk, build_keys, 0)

    one = jnp.ones((blk, PAIR), BF16)
    zero = jnp.zeros((blk, PAIR), BF16)
    q_ones = (jnp.where(lane < HEAD_DIM + _KBIAS_TERMS, one, zero),
              jnp.where(lane < _KBIAS_TERMS, one, zero))
    key_idx = lax.broadcasted_iota(jnp.int32, (blk, blk), 0)
    qry_idx = lax.broadcasted_iota(jnp.int32, (blk, blk), 1)
    causal = key_idx <= qry_idx

    def q_block(qi, _):
        q0 = pl.multiple_of(qi * blk, blk)
        q = q_ref[0, pl.ds(q0, blk), :]
        qa = (jnp.where(lo_half, q, q_ones[0]), jnp.where(lo_half, q_ones[1], q))
        cq = cumt_ref[0, 0, :, pl.ds(q0, blk)] * LOG2E
        m_ref[...] = jnp.full(m_ref.shape, NEG_BIG, F32)
        l_ref[...] = jnp.zeros_like(l_ref)
        acc_ref[...] = jnp.zeros_like(acc_ref)

        def kv_step(j, masked):
            k0 = pl.multiple_of(j * blk, blk)
            for h in range(2):
                st = lax.dot_general(kaug_ref[h, pl.ds(k0, blk), :], qa[h],
                                     (((1,), (1,)), ((), ())), preferred_element_type=F32)
                if masked:
                    st = jnp.where(causal, st, NEG_BIG)
                cqh = cq[h:h + 1, :]
                m_prev = m_ref[h]
                m_new = jnp.maximum(m_prev, jnp.max(st, axis=0, keepdims=True) + cqh)
                pt = jnp.exp2(st - (m_new - cqh))
                alpha = jnp.exp2(m_prev - m_new)
                l_ref[h] = alpha * l_ref[h] + jnp.sum(pt, axis=0, keepdims=True)
                vt = vt_ref[h * HEAD_DIM:(h + 1) * HEAD_DIM, pl.ds(k0, blk)]
                acc_ref[h] = alpha * acc_ref[h] + jnp.dot(
                    vt, pt.astype(BF16), preferred_element_type=F32)
                m_ref[h] = m_new

        def off_diag(j, _):
            kv_step(j, False)
            return 0

        lax.fori_loop(0, qi, off_diag, 0)
        kv_step(qi, True)

        out_t = jnp.concatenate(
            [acc_ref[h] * (1.0 / l_ref[h]) for h in range(2)], axis=0)
        o_ref[0, pl.ds(q0, blk), :] = out_t.T.astype(BF16)
        return 0

    lax.fori_loop(0, nblk, q_block, 0)


def _fox_attention(q, k, vt, kbias, cumt_pairs, blk):
    b, s, _ = q.shape
    n_pairs = FOX_HEADS // 2
    pair_blk = pl.BlockSpec((1, s, PAIR), lambda i, j: (i, 0, j))
    return pl.pallas_call(
        functools.partial(_fox_body, blk=blk),
        grid=(b, n_pairs),
        in_specs=[pair_blk, pair_blk,
                  pl.BlockSpec((PAIR, s), lambda i, j: (j, i)),
                  pair_blk,
                  pl.BlockSpec((1, 1, 2, s), lambda i, j: (i, j, 0, 0))],
        out_specs=pair_blk,
        out_shape=jax.ShapeDtypeStruct((b, s, FOX_WIDTH), BF16),
        scratch_shapes=[
            pltpu.VMEM((2, s, PAIR), BF16),
            pltpu.VMEM((2, 1, blk), F32),
            pltpu.VMEM((2, 1, blk), F32),
            pltpu.VMEM((2, HEAD_DIM, blk), F32),
        ],
        compiler_params=pltpu.CompilerParams(
            dimension_semantics=("arbitrary", "arbitrary"), vmem_limit_bytes=_VMEM_LIMIT),
        name="fox_attention",
    )(q, k, vt, kbias, cumt_pairs)


def _ffn_body(x_ref, ys_ref, yf_ref, wo_ref, nw_ref, wup_ref, cw_ref, cb_ref, wdn_ref,
              o_ref, carry_ref, hg_ref, hv_ref, act_ref, *, tm, ch):
    @pl.when(pl.program_id(1) == 0)
    def _():
        carry_ref[...] = jnp.zeros_like(carry_ref)

    x1 = (x_ref[0]
          + jnp.dot(ys_ref[0], wo_ref[0:SSD_WIDTH, :], preferred_element_type=F32)
          + jnp.dot(yf_ref[0], wo_ref[SSD_WIDTH:, :], preferred_element_type=F32))
    o_ref[0] = x1
    ms = jnp.mean(x1 * x1, axis=-1, keepdims=True)
    hf = (x1 * lax.rsqrt(ms + NORM_EPS) * nw_ref[...]).astype(BF16)

    hist = SUBLANES

    def conv_half(h_ref, c0):
        cs = slice(c0, c0 + ch)
        h_ref[0:hist, :] = carry_ref[:, cs]
        h_ref[hist:hist + tm, :] = jnp.dot(hf, wup_ref[:, cs], preferred_element_type=F32)
        carry_ref[:, cs] = h_ref[tm:tm + hist, :]
        out = cb_ref[:, cs] + cw_ref[FFN_CONV - 1:FFN_CONV, cs] * h_ref[hist:hist + tm, :]
        for k in range(FFN_CONV - 1):
            off = hist - (FFN_CONV - 1) + k
            out = out + cw_ref[k:k + 1, cs] * h_ref[off:off + tm, :]
        return out

    for c0 in range(0, D_FF, ch):
        gate = conv_half(hg_ref, c0)
        val = conv_half(hv_ref, D_FF + c0)
        act_ref[:, c0:c0 + ch] = (gate * (1.0 / (1.0 + jnp.exp(-gate))) * val).astype(BF16)

    o_ref[0] = o_ref[0] + jnp.dot(act_ref[...], wdn_ref[...], preferred_element_type=F32)


def _out_ffn(x, y_ssd, y_fox, w_out, norm_w, w_up, conv_w, conv_b, w_down, tm, ch):
    b, s, _ = x.shape
    blk = lambda n: pl.BlockSpec((1, tm, n), lambda i, j: (i, j, 0))
    return pl.pallas_call(
        functools.partial(_ffn_body, tm=tm, ch=ch),
        grid=(b, s // tm),
        in_specs=[blk(D_MODEL), blk(SSD_WIDTH), blk(FOX_WIDTH),
                  _const_spec(w_out.shape), _const_spec(norm_w.shape), _const_spec(w_up.shape),
                  _const_spec(conv_w.shape), _const_spec(conv_b.shape), _const_spec(w_down.shape)],
        out_specs=blk(D_MODEL),
        out_shape=jax.ShapeDtypeStruct((b, s, D_MODEL), F32),
        scratch_shapes=[
            pltpu.VMEM((SUBLANES, 2 * D_FF), F32),
            pltpu.VMEM((tm + SUBLANES, ch), F32),
            pltpu.VMEM((tm + SUBLANES, ch), F32),
            pltpu.VMEM((tm, D_FF), BF16),
        ],
        compiler_params=pltpu.CompilerParams(
            dimension_semantics=("arbitrary", "arbitrary"), vmem_limit_bytes=_VMEM_LIMIT),
        name="out_ffn",
    )(x, y_ssd, y_fox, w_out, norm_w, w_up, conv_w, conv_b, w_down)


def _pad_lanes(v):
    return jnp.pad(v.astype(F32), (0, LANES - v.shape[0]))[None, :]


def _layer(x, norm_mix_w, w_in, ssd_conv_w, ssd_conv_b, ssd_dt_bias, ssd_a_log, ssd_d, ssd_norm_w,
           fox_f_bias, fox_q_norm_w, fox_k_norm_w, w_out, norm_ffn_w, w_up, ffn_conv_w, ffn_conv_b,
           w_down, *, tm_in, attn_blk, tm_ffn, ffn_ch):
    b, s, d = x.shape
    z_end = SSD_WIDTH
    xbc_end = z_end + SSD_XBC_WIDTH
    dt_end = xbc_end + SSD_HEADS
    q_end = dt_end + FOX_WIDTH
    k_end = q_end + FOX_WIDTH
    v_end = k_end + FOX_WIDTH

    def pad_cols(w):
        return jnp.pad(w, ((0, 0), (0, LANES - w.shape[1])))

    w_r = jnp.concatenate(
        [w_in[:, :xbc_end], w_in[:, dt_end:v_end],
         pad_cols(w_in[:, xbc_end:dt_end]), pad_cols(w_in[:, v_end:])], axis=1).astype(BF16)
    heads_per_tile = MXU_DIM // HEAD_DIM
    g = jnp.kron(jnp.eye(heads_per_tile, dtype=F32), jnp.ones((HEAD_DIM, HEAD_DIM), F32)).astype(BF16)
    qw = (jnp.tile(fox_q_norm_w.astype(F32), heads_per_tile) * (HEAD_DIM ** -0.5 * LOG2E))[None, :]
    kw = jnp.tile(fox_k_norm_w.astype(F32), heads_per_tile)[None, :]

    z, xbc, q, k, vt, dt_raw, f_raw = _in_projection(
        x.reshape(b * s, d), norm_mix_w[None, :].astype(F32), w_r, g, qw, kw, tm_in)
    r3 = lambda a: a.reshape(b, s, a.shape[-1])

    y_ssd, kbias, cumt = _ssd_mixer(
        r3(xbc), r3(z), r3(dt_raw), r3(f_raw),
        ssd_conv_w.astype(F32), ssd_conv_b[None, :].astype(F32),
        _pad_lanes(ssd_dt_bias), _pad_lanes(ssd_a_log),
        jnp.repeat(ssd_d.astype(F32), HEAD_DIM)[None, :], ssd_norm_w[None, :].astype(F32),
        _pad_lanes(fox_f_bias))

    cumt_pairs = cumt[:, :FOX_HEADS, :].reshape(b, FOX_HEADS // 2, 2, s)
    y_fox = _fox_attention(r3(q), r3(k), vt, kbias, cumt_pairs, attn_blk)

    return _out_ffn(x, y_ssd, y_fox, w_out.astype(BF16), norm_ffn_w[None, :].astype(F32),
                    w_up.astype(BF16), ffn_conv_w.astype(F32), ffn_conv_b[None, :].astype(F32),
                    w_down.astype(BF16), tm_ffn, ffn_ch)


def kernel(x, norm_mix_w, w_in, ssd_conv_w, ssd_conv_b, ssd_dt_bias, ssd_a_log, ssd_d, ssd_norm_w,
           fox_f_bias, fox_q_norm_w, fox_k_norm_w, w_out, norm_ffn_w, w_up, ffn_conv_w, ffn_conv_b,
           w_down):
    depth = w_in.shape[0]
    for layer in range(depth):
        x = _layer(x, norm_mix_w[layer], w_in[layer], ssd_conv_w[layer], ssd_conv_b[layer],
                   ssd_dt_bias[layer], ssd_a_log[layer], ssd_d[layer], ssd_norm_w[layer],
                   fox_f_bias[layer], fox_q_norm_w[layer], fox_k_norm_w[layer], w_out[layer],
                   norm_ffn_w[layer], w_up[layer], ffn_conv_w[layer], ffn_conv_b[layer],
                   w_down[layer], tm_in=512, attn_blk=512, tm_ffn=512, ffn_ch=256)
    return x
```

```python
import functools
import math

import jax
import jax.numpy as jnp
from jax import lax
from jax.experimental import pallas as pl
from jax.experimental.pallas import tpu as pltpu

F32 = jnp.float32
BF16 = jnp.bfloat16

D_MODEL = 1024
HEAD_DIM = 64
SSD_HEADS = 16
SSD_GROUPS = 2
SSD_STATE = 128
SSD_CONV = 4
SSD_CHUNK = 128
SSD_WIDTH = SSD_HEADS * HEAD_DIM
SSD_BC_WIDTH = SSD_GROUPS * SSD_STATE
SSD_XBC_WIDTH = SSD_WIDTH + 2 * SSD_BC_WIDTH
FOX_HEADS = 16
FOX_WIDTH = FOX_HEADS * HEAD_DIM
D_FF = 2816
FFN_CONV = 3
NORM_EPS = 1e-6
LOG2E = math.log2(math.e)

LANES = 128
SUBLANES = 8
MXU_DIM = 256
PAIR = 2 * HEAD_DIM
NEG_BIG = -1e30

_Z0 = 0
_XBC0 = _Z0 + SSD_WIDTH
_Q0 = _XBC0 + SSD_XBC_WIDTH
_K0 = _Q0 + FOX_WIDTH
_V0 = _K0 + FOX_WIDTH
_DT0 = _V0 + FOX_WIDTH
_F0 = _DT0 + LANES
_IN_COLS_PADDED = _F0 + LANES

_VMEM_LIMIT = 56 * 1024 * 1024


def _const_spec(shape):
    zeros = (0,) * len(shape)
    return pl.BlockSpec(shape, lambda *_: zeros, pipeline_mode=pl.Buffered(1))


def _inproj_body(x_ref, nw_ref, w_ref, g_ref, qw_ref, kw_ref,
                 z_ref, xbc_ref, q_ref, k_ref, vt_ref, dt_ref, f_ref, vtmp_ref):
    x = x_ref[...]
    ms = jnp.mean(x * x, axis=-1, keepdims=True)
    h = (x * lax.rsqrt(ms + NORM_EPS) * nw_ref[...]).astype(BF16)

    def proj(c0, n):
        return jnp.dot(h, w_ref[:, c0:c0 + n], preferred_element_type=F32)

    z_ref[...] = proj(_Z0, SSD_WIDTH).astype(BF16)
    xbc_ref[...] = proj(_XBC0, SSD_XBC_WIDTH).astype(BF16)

    vtmp_ref[...] = proj(_V0, FOX_WIDTH)
    for c in range(0, FOX_WIDTH, MXU_DIM):
        vt_ref[c:c + MXU_DIM, :] = vtmp_ref[:, c:c + MXU_DIM].T.astype(BF16)

    g = g_ref[...]
    for base, o_ref, hw_ref in ((_Q0, q_ref, qw_ref), (_K0, k_ref, kw_ref)):
        vtmp_ref[...] = proj(base, FOX_WIDTH)
        for c in range(0, FOX_WIDTH, MXU_DIM):
            t = vtmp_ref[:, c:c + MXU_DIM]
            ss = jnp.dot((t * t).astype(BF16), g, preferred_element_type=F32)
            o_ref[:, c:c + MXU_DIM] = (
                t * lax.rsqrt(ss * (1.0 / HEAD_DIM) + NORM_EPS) * hw_ref[...]).astype(BF16)

    dtf = proj(_DT0, 2 * LANES)
    dt_ref[...] = dtf[:, :LANES]
    f_ref[...] = dtf[:, LANES:]


def _in_projection(x2, norm_w, w_r, g, qw, kw, tm):
    t = x2.shape[0]
    row = lambda n: pl.BlockSpec((tm, n), lambda i: (i, 0))
    out_shape = (
        jax.ShapeDtypeStruct((t, SSD_WIDTH), BF16),
        jax.ShapeDtypeStruct((t, SSD_XBC_WIDTH), BF16),
        jax.ShapeDtypeStruct((t, FOX_WIDTH), BF16),
        jax.ShapeDtypeStruct((t, FOX_WIDTH), BF16),
        jax.ShapeDtypeStruct((FOX_WIDTH, t), BF16),
        jax.ShapeDtypeStruct((t, LANES), F32),
        jax.ShapeDtypeStruct((t, LANES), F32),
    )
    return pl.pallas_call(
        _inproj_body,
        grid=(t // tm,),
        in_specs=[row(D_MODEL), _const_spec((1, D_MODEL)), _const_spec(w_r.shape),
                  _const_spec(g.shape), _const_spec(qw.shape), _const_spec(kw.shape)],
        out_specs=(row(SSD_WIDTH), row(SSD_XBC_WIDTH), row(FOX_WIDTH), row(FOX_WIDTH),
                   pl.BlockSpec((FOX_WIDTH, tm), lambda i: (0, i)), row(LANES), row(LANES)),
        out_shape=out_shape,
        scratch_shapes=[pltpu.VMEM((tm, FOX_WIDTH), F32)],
        compiler_params=pltpu.CompilerParams(
            dimension_semantics=("arbitrary",), vmem_limit_bytes=_VMEM_LIMIT),
        name="in_projection",
    )(x2, norm_w, w_r, g, qw, kw)


def _softplus(x):
    return jnp.maximum(x, 0.0) + jnp.log1p(jnp.exp(-jnp.abs(x)))


def _split3(v):
    hi = v.astype(BF16)
    r1 = v - hi.astype(F32)
    mid = r1.astype(BF16)
    lo = (r1 - mid.astype(F32)).astype(BF16)
    return hi, mid, lo


def _ssd_body(xbc_ref, z_ref, dt_ref, f_ref, cw_ref, cb_ref, dtb_ref, alog_ref, dskip_ref,
              nw_ref, fb_ref, sel_ref,
              y_ref, kbias_ref, cumt_ref,
              state_ref, xext_ref, cumcarry_ref, xdd_ref, yg_ref):
    L = SSD_CHUNK
    c = pl.program_id(1)

    @pl.when(c == 0)
    def _():
        state_ref[...] = jnp.zeros_like(state_ref)
        xext_ref[0:SUBLANES, :] = jnp.zeros((SUBLANES, SSD_XBC_WIDTH), F32)
        cumcarry_ref[...] = jnp.zeros_like(cumcarry_ref)

    xext_ref[SUBLANES:SUBLANES + L, :] = xbc_ref[0].astype(F32)
    conv = cb_ref[...] + cw_ref[3:4, :] * xext_ref[SUBLANES:SUBLANES + L, :]
    for k in range(SSD_CONV - 1):
        off = SUBLANES - (SSD_CONV - 1) + k
        conv = conv + cw_ref[k:k + 1, :] * xext_ref[off:off + L, :]
    xext_ref[0:SUBLANES, :] = xext_ref[L:L + SUBLANES, :]
    xbc = conv * (1.0 / (1.0 + jnp.exp(-conv)))

    xs = xbc[:, :SSD_WIDTH]
    bm = xbc[:, SSD_WIDTH:SSD_WIDTH + SSD_BC_WIDTH]
    cm = xbc[:, SSD_WIDTH + SSD_BC_WIDTH:]

    dt = _softplus(dt_ref[0] + dtb_ref[...])
    da = dt * (-jnp.exp(alog_ref[...]))
    logf = -_softplus(-(f_ref[0] + fb_ref[...]))

    rows = lax.broadcasted_iota(jnp.int32, (L, L), 0)
    cols = lax.broadcasted_iota(jnp.int32, (L, L), 1)
    tri = rows >= cols
    tri_b = jnp.where(tri, 1.0, 0.0).astype(BF16)
    parts = jnp.concatenate(_split3(da) + _split3(logf), axis=1)
    sums = jnp.dot(tri_b, parts, preferred_element_type=F32)
    a_cs = sums[:, 0:LANES] + sums[:, LANES:2 * LANES] + sums[:, 2 * LANES:3 * LANES]
    cum = (sums[:, 3 * LANES:4 * LANES] + sums[:, 4 * LANES:5 * LANES]
           + sums[:, 5 * LANES:6 * LANES]) + cumcarry_ref[0:1, :]
    cumcarry_ref[0:1, :] = cum[L - 1:L, :]
    cumt_ref[0] = cum.T
    kb = jnp.concatenate(_split3(cum * (-LOG2E)), axis=1)
    kbias_ref[0] = jnp.dot(kb, sel_ref[...], preferred_element_type=F32).astype(BF16)

    a_cst = a_cs.T
    ea = jnp.exp(a_cs)
    a_end = a_cs[L - 1:L, :]
    dte = jnp.exp(a_end - a_cs)
    cd = jnp.exp(a_end)

    lane = lax.broadcasted_iota(jnp.int32, (L, PAIR), 1)
    lo_half = lane < HEAD_DIM
    heads_per_group = SSD_HEADS // SSD_GROUPS
    pairs_per_group = heads_per_group // 2

    cbs, cms, bms = [], [], []
    for g in range(SSD_GROUPS):
        cm_g = cm[:, g * SSD_STATE:(g + 1) * SSD_STATE]
        bm_g = bm[:, g * SSD_STATE:(g + 1) * SSD_STATE]
        cbs.append(lax.dot_general(cm_g.astype(BF16), bm_g.astype(BF16),
                                   (((1,), (1,)), ((), ())), preferred_element_type=F32))
        cms.append(cm_g)
        bms.append(bm_g)

    ssq = [None] * SSD_GROUPS
    cd_parts = []
    for p in range(SSD_HEADS // 2):
        g = p // pairs_per_group
        h0, h1 = 2 * p, 2 * p + 1
        sl = slice(p * PAIR, (p + 1) * PAIR)
        xs_p = xs[:, sl]
        xdt_p = xs_p * jnp.where(lo_half, dt[:, h0:h0 + 1], dt[:, h1:h1 + 1])
        xdd_ref[:, sl] = (xdt_p * jnp.where(lo_half, dte[:, h0:h0 + 1], dte[:, h1:h1 + 1])).astype(BF16)
        cd_parts.append(jnp.where(lo_half[0:1, :], cd[:, h0:h0 + 1], cd[:, h1:h1 + 1]))
        rhs = jnp.concatenate([xdt_p.astype(BF16), state_ref[:, sl].astype(BF16)], axis=0)
        ys = []
        for h in (h0, h1):
            seg = a_cs[:, h:h + 1] - a_cst[h:h + 1, :]
            decay = jnp.exp(jnp.where(tri, seg, NEG_BIG))
            lhs = jnp.concatenate([(cbs[g] * decay).astype(BF16),
                                   (cms[g] * ea[:, h:h + 1]).astype(BF16)], axis=1)
            ys.append(jnp.dot(lhs, rhs, preferred_element_type=F32))
        y_p = jnp.where(lo_half, ys[0], ys[1]) + dskip_ref[:, sl] * xs_p
        zp = z_ref[0, :, sl].astype(F32)
        yg = y_p * (zp * (1.0 / (1.0 + jnp.exp(-zp))))
        yg_ref[:, sl] = yg
        s2 = jnp.sum(yg * yg, axis=-1, keepdims=True)
        ssq[g] = s2 if ssq[g] is None else ssq[g] + s2

    gw = SSD_WIDTH // SSD_GROUPS
    cd_row = jnp.concatenate(cd_parts, axis=1)
    for g in range(SSD_GROUPS):
        gs = slice(g * gw, (g + 1) * gw)
        scale = lax.rsqrt(ssq[g] * (1.0 / gw) + NORM_EPS)
        y_ref[0, :, gs] = (yg_ref[:, gs] * scale * nw_ref[:, gs]).astype(BF16)
        upd = jnp.dot(bms[g].T.astype(BF16), xdd_ref[:, gs], preferred_element_type=F32)
        state_ref[:, gs] = state_ref[:, gs] * cd_row[:, gs] + upd


_KBIAS_TERMS = 3


def _kbias_selector():
    import numpy as np
    sel = np.zeros((_KBIAS_TERMS * LANES, FOX_WIDTH), np.float32)
    for h in range(FOX_HEADS):
        base = (h // 2) * PAIR + (HEAD_DIM if h % 2 == 0 else 0)
        for i in range(_KBIAS_TERMS):
            sel[i * LANES + h, base + i] = 1.0
    return jnp.asarray(sel, BF16)


def _ssd_mixer(xbc, z, dt_raw, f_raw, conv_w, conv_b, dt_bias_p, alog_p, dskip_e, norm_w, fbias_p):
    b, s, _ = xbc.shape
    L = SSD_CHUNK
    sel = _kbias_selector()
    blk = lambda n: pl.BlockSpec((1, L, n), lambda i, j: (i, j, 0))
    return pl.pallas_call(
        _ssd_body,
        grid=(b, s // L),
        in_specs=[blk(SSD_XBC_WIDTH), blk(SSD_WIDTH), blk(LANES), blk(LANES),
                  _const_spec(conv_w.shape), _const_spec(conv_b.shape), _const_spec(dt_bias_p.shape),
                  _const_spec(alog_p.shape), _const_spec(dskip_e.shape), _const_spec(norm_w.shape),
                  _const_spec(fbias_p.shape), _const_spec(sel.shape)],
        out_specs=(blk(SSD_WIDTH), blk(FOX_WIDTH),
                   pl.BlockSpec((1, LANES, L), lambda i, j: (i, 0, j))),
        out_shape=(jax.ShapeDtypeStruct((b, s, SSD_WIDTH), BF16),
                   jax.ShapeDtypeStruct((b, s, FOX_WIDTH), BF16),
                   jax.ShapeDtypeStruct((b, LANES, s), F32)),
        scratch_shapes=[
            pltpu.VMEM((SSD_STATE, SSD_WIDTH), F32),
            pltpu.VMEM((L + SUBLANES, SSD_XBC_WIDTH), F32),
            pltpu.VMEM((SUBLANES, LANES), F32),
            pltpu.VMEM((L, SSD_WIDTH), BF16),
            pltpu.VMEM((L, SSD_WIDTH), F32),
        ],
        compiler_params=pltpu.CompilerParams(
            dimension_semantics=("arbitrary", "arbitrary"), vmem_limit_bytes=_VMEM_LIMIT),
        name="ssd_mixer",
    )(xbc, z, dt_raw, f_raw, conv_w, conv_b, dt_bias_p, alog_p, dskip_e, norm_w, fbias_p, sel)


_VAUG_ROWS = HEAD_DIM + 16


def _fox_body(q_ref, k_ref, vt_ref, kbias_ref, cumt_ref, o_ref,
              kaug_ref, vaug_ref, m_ref, acc_ref, st_a, st_b, *, blk):
    s = q_ref.shape[1]
    nblk = s // blk
    lane = lax.broadcasted_iota(jnp.int32, (blk, PAIR), 1).astype(F32).astype(BF16)
    lo_half = lane < HEAD_DIM

    def build_keys(i, _):
        r0 = pl.multiple_of(i * blk, blk)
        kp = k_ref[0, pl.ds(r0, blk), :]
        kb = kbias_ref[0, pl.ds(r0, blk), :]
        kaug_ref[0, pl.ds(r0, blk), :] = jnp.where(lo_half, kp, kb)
        kaug_ref[1, pl.ds(r0, blk), :] = jnp.where(lo_half, kb, kp)
        for h in range(2):
            vaug_ref[h, 0:HEAD_DIM, pl.ds(r0, blk)] = vt_ref[h * HEAD_DIM:(h + 1) * HEAD_DIM,
                                                             pl.ds(r0, blk)]
            vaug_ref[h, HEAD_DIM:_VAUG_ROWS, pl.ds(r0, blk)] = jnp.ones(
                (_VAUG_ROWS - HEAD_DIM, blk), BF16)
        return 0

    lax.fori_loop(0, nblk, build_keys, 0)

    one = jnp.ones((blk, PAIR), BF16)
    zero = jnp.zeros((blk, PAIR), BF16)
    q_ones = (jnp.where(lane < HEAD_DIM + _KBIAS_TERMS, one, zero),
              jnp.where(lane < _KBIAS_TERMS, one, zero))
    key_idx = lax.broadcasted_iota(jnp.int32, (blk, blk), 0)
    qry_idx = lax.broadcasted_iota(jnp.int32, (blk, blk), 1)
    causal = key_idx <= qry_idx

    def q_block(qi, _):
        q0 = pl.multiple_of(qi * blk, blk)
        q = q_ref[0, pl.ds(q0, blk), :]
        qa = (jnp.where(lo_half, q, q_ones[0]), jnp.where(lo_half, q_ones[1], q))
        cq = cumt_ref[0, 0, :, pl.ds(q0, blk)] * LOG2E
        m_ref[...] = jnp.full(m_ref.shape, NEG_BIG, F32)
        acc_ref[...] = jnp.zeros_like(acc_ref)

        def scores(h, j, st_ref):
            k0 = pl.multiple_of(j * blk, blk)
            st_ref[...] = lax.dot_general(kaug_ref[h, pl.ds(k0, blk), :], qa[h],
                                          (((1,), (1,)), ((), ())), preferred_element_type=F32)

        def softmax_pv(h, j, st_ref, masked):
            k0 = pl.multiple_of(j * blk, blk)
            st = st_ref[...]
            if masked:
                st = jnp.where(causal, st, NEG_BIG)
            cqh = cq[h:h + 1, :]
            m_prev = m_ref[h]
            m_new = jnp.maximum(m_prev, jnp.max(st, axis=0, keepdims=True) + cqh)
            pt = jnp.exp2(st - (m_new - cqh)).astype(BF16)
            alpha = jnp.exp2(m_prev - m_new)
            acc_ref[h] = alpha * acc_ref[h] + jnp.dot(
                vaug_ref[h, :, pl.ds(k0, blk)], pt, preferred_element_type=F32)
            m_ref[h] = m_new

        scores(0, 0, st_a)

        def off_diag(j, _):
            scores(1, j, st_b)
            softmax_pv(0, j, st_a, False)
            scores(0, j + 1, st_a)
            softmax_pv(1, j, st_b, False)
            return 0

        lax.fori_loop(0, qi, off_diag, 0)
        scores(1, qi, st_b)
        softmax_pv(0, qi, st_a, True)
        softmax_pv(1, qi, st_b, True)

        out_t = jnp.concatenate(
            [acc_ref[h, 0:HEAD_DIM, :] * (1.0 / acc_ref[h, HEAD_DIM:HEAD_DIM + 1, :])
             for h in range(2)], axis=0)
        o_ref[0, pl.ds(q0, blk), :] = out_t.T.astype(BF16)
        return 0

    lax.fori_loop(0, nblk, q_block, 0)


def _fox_attention(q, k, vt, kbias, cumt_pairs, blk):
    b, s, _ = q.shape
    n_pairs = FOX_HEADS // 2
    pair_blk = pl.BlockSpec((1, s, PAIR), lambda i, j: (i, 0, j))
    return pl.pallas_call(
        functools.partial(_fox_body, blk=blk),
        grid=(b, n_pairs),
        in_specs=[pair_blk, pair_blk,
                  pl.BlockSpec((PAIR, s), lambda i, j: (j, i)),
                  pair_blk,
                  pl.BlockSpec((1, 1, 2, s), lambda i, j: (i, j, 0, 0))],
        out_specs=pair_blk,
        out_shape=jax.ShapeDtypeStruct((b, s, FOX_WIDTH), BF16),
        scratch_shapes=[
            pltpu.VMEM((2, s, PAIR), BF16),
            pltpu.VMEM((2, _VAUG_ROWS, s), BF16),
            pltpu.VMEM((2, 1, blk), F32),
            pltpu.VMEM((2, _VAUG_ROWS, blk), F32),
            pltpu.VMEM((blk, blk), F32),
            pltpu.VMEM((blk, blk), F32),
        ],
        compiler_params=pltpu.CompilerParams(
            dimension_semantics=("arbitrary", "arbitrary"), vmem_limit_bytes=_VMEM_LIMIT),
        name="fox_attention",
    )(q, k, vt, kbias, cumt_pairs)


def _ffn_body(x_ref, ys_ref, yf_ref, wo_ref, nw_ref, wup_ref, cw_ref, cb_ref, wdn_ref,
              o_ref, carry_ref, hg_ref, hv_ref, act_ref, *, tm, ch):
    @pl.when(pl.program_id(1) == 0)
    def _():
        carry_ref[...] = jnp.zeros_like(carry_ref)

    x1 = (x_ref[0]
          + jnp.dot(ys_ref[0], wo_ref[0:SSD_WIDTH, :], preferred_element_type=F32)
          + jnp.dot(yf_ref[0], wo_ref[SSD_WIDTH:, :], preferred_element_type=F32))
    o_ref[0] = x1
    ms = jnp.mean(x1 * x1, axis=-1, keepdims=True)
    hf = (x1 * lax.rsqrt(ms + NORM_EPS) * nw_ref[...]).astype(BF16)

    hist = SUBLANES

    def conv_half(h_ref, c0):
        cs = slice(c0, c0 + ch)
        h_ref[0:hist, :] = carry_ref[:, cs]
        h_ref[hist:hist + tm, :] = jnp.dot(hf, wup_ref[:, cs], preferred_element_type=F32)
        carry_ref[:, cs] = h_ref[tm:tm + hist, :]
        out = cb_ref[:, cs] + cw_ref[FFN_CONV - 1:FFN_CONV, cs] * h_ref[hist:hist + tm, :]
        for k in range(FFN_CONV - 1):
            off = hist - (FFN_CONV - 1) + k
            out = out + cw_ref[k:k + 1, cs] * h_ref[off:off + tm, :]
        return out

    for c0 in range(0, D_FF, ch):
        gate = conv_half(hg_ref, c0)
        val = conv_half(hv_ref, D_FF + c0)
        act_ref[:, c0:c0 + ch] = (gate * (1.0 / (1.0 + jnp.exp(-gate))) * val).astype(BF16)

    o_ref[0] = o_ref[0] + jnp.dot(act_ref[...], wdn_ref[...], preferred_element_type=F32)


def _out_ffn(x, y_ssd, y_fox, w_out, norm_w, w_up, conv_w, conv_b, w_down, tm, ch):
    b, s, _ = x.shape
    blk = lambda n: pl.BlockSpec((1, tm, n), lambda i, j: (i, j, 0))
    return pl.pallas_call(
        functools.partial(_ffn_body, tm=tm, ch=ch),
        grid=(b, s // tm),
        in_specs=[blk(D_MODEL), blk(SSD_WIDTH), blk(FOX_WIDTH),
                  _const_spec(w_out.shape), _const_spec(norm_w.shape), _const_spec(w_up.shape),
                  _const_spec(conv_w.shape), _const_spec(conv_b.shape), _const_spec(w_down.shape)],
        out_specs=blk(D_MODEL),
        out_shape=jax.ShapeDtypeStruct((b, s, D_MODEL), F32),
        scratch_shapes=[
            pltpu.VMEM((SUBLANES, 2 * D_FF), F32),
            pltpu.VMEM((tm + SUBLANES, ch), F32),
            pltpu.VMEM((tm + SUBLANES, ch), F32),
            pltpu.VMEM((tm, D_FF), BF16),
        ],
        compiler_params=pltpu.CompilerParams(
            dimension_semantics=("arbitrary", "arbitrary"), vmem_limit_bytes=_VMEM_LIMIT),
        name="out_ffn",
    )(x, y_ssd, y_fox, w_out, norm_w, w_up, conv_w, conv_b, w_down)


def _pad_lanes(v):
    return jnp.pad(v.astype(F32), (0, LANES - v.shape[0]))[None, :]


def _layer(x, norm_mix_w, w_in, ssd_conv_w, ssd_conv_b, ssd_dt_bias, ssd_a_log, ssd_d, ssd_norm_w,
           fox_f_bias, fox_q_norm_w, fox_k_norm_w, w_out, norm_ffn_w, w_up, ffn_conv_w, ffn_conv_b,
           w_down, *, tm_in, attn_blk, tm_ffn, ffn_ch):
    b, s, d = x.shape
    z_end = SSD_WIDTH
    xbc_end = z_end + SSD_XBC_WIDTH
    dt_end = xbc_end + SSD_HEADS
    q_end = dt_end + FOX_WIDTH
    k_end = q_end + FOX_WIDTH
    v_end = k_end + FOX_WIDTH

    def pad_cols(w):
        return jnp.pad(w, ((0, 0), (0, LANES - w.shape[1])))

    w_r = jnp.concatenate(
        [w_in[:, :xbc_end], w_in[:, dt_end:v_end],
         pad_cols(w_in[:, xbc_end:dt_end]), pad_cols(w_in[:, v_end:])], axis=1).astype(BF16)
    heads_per_tile = MXU_DIM // HEAD_DIM
    g = jnp.kron(jnp.eye(heads_per_tile, dtype=F32), jnp.ones((HEAD_DIM, HEAD_DIM), F32)).astype(BF16)
    qw = (jnp.tile(fox_q_norm_w.astype(F32), heads_per_tile) * (HEAD_DIM ** -0.5 * LOG2E))[None, :]
    kw = jnp.tile(fox_k_norm_w.astype(F32), heads_per_tile)[None, :]

    z, xbc, q, k, vt, dt_raw, f_raw = _in_projection(
        x.reshape(b * s, d), norm_mix_w[None, :].astype(F32), w_r, g, qw, kw, tm_in)
    r3 = lambda a: a.reshape(b, s, a.shape[-1])

    y_ssd, kbias, cumt = _ssd_mixer(
        r3(xbc), r3(z), r3(dt_raw), r3(f_raw),
        ssd_conv_w.astype(F32), ssd_conv_b[None, :].astype(F32),
        _pad_lanes(ssd_dt_bias), _pad_lanes(ssd_a_log),
        jnp.repeat(ssd_d.astype(F32), HEAD_DIM)[None, :], ssd_norm_w[None, :].astype(F32),
        _pad_lanes(fox_f_bias))

    cumt_pairs = cumt[:, :FOX_HEADS, :].reshape(b, FOX_HEADS // 2, 2, s)
    y_fox = _fox_attention(r3(q), r3(k), vt, kbias, cumt_pairs, attn_blk)

    return _out_ffn(x, y_ssd, y_fox, w_out.astype(BF16), norm_ffn_w[None, :].astype(F32),
                    w_up.astype(BF16), ffn_conv_w.astype(F32), ffn_conv_b[None, :].astype(F32),
                    w_down.astype(BF16), tm_ffn, ffn_ch)


def kernel(x, norm_mix_w, w_in, ssd_conv_w, ssd_conv_b, ssd_dt_bias, ssd_a_log, ssd_d, ssd_norm_w,
           fox_f_bias, fox_q_norm_w, fox_k_norm_w, w_out, norm_ffn_w, w_up, ffn_conv_w, ffn_conv_b,
           w_down):
    depth = w_in.shape[0]
    for layer in range(depth):
        x = _layer(x, norm_mix_w[layer], w_in[layer], ssd_conv_w[layer], ssd_conv_b[layer],
                   ssd_dt_bias[layer], ssd_a_log[layer], ssd_d[layer], ssd_norm_w[layer],
                   fox_f_bias[layer], fox_q_norm_w[layer], fox_k_norm_w[layer], w_out[layer],
                   norm_ffn_w[layer], w_up[layer], ffn_conv_w[layer], ffn_conv_b[layer],
                   w_down[layer], tm_in=512, attn_blk=512, tm_ffn=512, ffn_ch=256)
    return x
```

```python
import functools
import math

import jax
import jax.numpy as jnp
from jax import lax
from jax.experimental import pallas as pl
from jax.experimental.pallas import tpu as pltpu

F32 = jnp.float32
BF16 = jnp.bfloat16

D_MODEL = 1024
HEAD_DIM = 64
SSD_HEADS = 16
SSD_GROUPS = 2
SSD_STATE = 128
SSD_CONV = 4
SSD_CHUNK = 128
SSD_WIDTH = SSD_HEADS * HEAD_DIM
SSD_BC_WIDTH = SSD_GROUPS * SSD_STATE
SSD_XBC_WIDTH = SSD_WIDTH + 2 * SSD_BC_WIDTH
FOX_HEADS = 16
FOX_WIDTH = FOX_HEADS * HEAD_DIM
D_FF = 2816
FFN_CONV = 3
NORM_EPS = 1e-6
LOG2E = math.log2(math.e)

LANES = 128
SUBLANES = 8
MXU_DIM = 256
PAIR = 2 * HEAD_DIM
NEG_BIG = -1e30

_Z0 = 0
_XBC0 = _Z0 + SSD_WIDTH
_Q0 = _XBC0 + SSD_XBC_WIDTH
_K0 = _Q0 + FOX_WIDTH
_V0 = _K0 + FOX_WIDTH
_GATE0 = _V0 + FOX_WIDTH
_IN_COLS_PADDED = _GATE0 + LANES

_VMEM_LIMIT = 56 * 1024 * 1024


def _const_spec(shape):
    zeros = (0,) * len(shape)
    return pl.BlockSpec(shape, lambda *_: zeros, pipeline_mode=pl.Buffered(1))


def _inproj_body(x_ref, nw_ref, w_ref, g_ref, qw_ref, kw_ref,
                 z_ref, xbc_ref, q_ref, k_ref, vt_ref, gate_ref, vtmp_ref):
    x = x_ref[...]
    ms = jnp.mean(x * x, axis=-1, keepdims=True)
    h = (x * lax.rsqrt(ms + NORM_EPS) * nw_ref[...]).astype(BF16)

    def proj(c0, n):
        return jnp.dot(h, w_ref[:, c0:c0 + n], preferred_element_type=F32)

    z_ref[...] = proj(_Z0, SSD_WIDTH).astype(BF16)
    xbc_ref[...] = proj(_XBC0, SSD_XBC_WIDTH).astype(BF16)

    vtmp_ref[...] = proj(_V0, FOX_WIDTH)
    for c in range(0, FOX_WIDTH, MXU_DIM):
        vt_ref[c:c + MXU_DIM, :] = vtmp_ref[:, c:c + MXU_DIM].T.astype(BF16)

    g = g_ref[...]
    for base, o_ref, hw_ref in ((_Q0, q_ref, qw_ref), (_K0, k_ref, kw_ref)):
        vtmp_ref[...] = proj(base, FOX_WIDTH)
        for c in range(0, FOX_WIDTH, MXU_DIM):
            t = vtmp_ref[:, c:c + MXU_DIM]
            ss = jnp.dot((t * t).astype(BF16), g, preferred_element_type=F32)
            o_ref[:, c:c + MXU_DIM] = (
                t * lax.rsqrt(ss * (1.0 / HEAD_DIM) + NORM_EPS) * hw_ref[...]).astype(BF16)

    gate_ref[...] = proj(_GATE0, LANES)


def _in_projection(x2, norm_w, w_r, g, qw, kw, tm):
    t = x2.shape[0]
    row = lambda n: pl.BlockSpec((tm, n), lambda i: (i, 0))
    out_shape = (
        jax.ShapeDtypeStruct((t, SSD_WIDTH), BF16),
        jax.ShapeDtypeStruct((t, SSD_XBC_WIDTH), BF16),
        jax.ShapeDtypeStruct((t, FOX_WIDTH), BF16),
        jax.ShapeDtypeStruct((t, FOX_WIDTH), BF16),
        jax.ShapeDtypeStruct((FOX_WIDTH, t), BF16),
        jax.ShapeDtypeStruct((t, LANES), F32),
    )
    return pl.pallas_call(
        _inproj_body,
        grid=(t // tm,),
        in_specs=[row(D_MODEL), _const_spec((1, D_MODEL)), _const_spec(w_r.shape),
                  _const_spec(g.shape), _const_spec(qw.shape), _const_spec(kw.shape)],
        out_specs=(row(SSD_WIDTH), row(SSD_XBC_WIDTH), row(FOX_WIDTH), row(FOX_WIDTH),
                   pl.BlockSpec((FOX_WIDTH, tm), lambda i: (0, i)), row(LANES)),
        out_shape=out_shape,
        scratch_shapes=[pltpu.VMEM((tm, FOX_WIDTH), F32)],
        compiler_params=pltpu.CompilerParams(
            dimension_semantics=("arbitrary",), vmem_limit_bytes=_VMEM_LIMIT),
        name="in_projection",
    )(x2, norm_w, w_r, g, qw, kw)


def _softplus(x):
    e = jnp.exp(-jnp.abs(x))
    u = 1.0 + e
    return jnp.maximum(x, 0.0) + (jnp.log(u) - ((u - 1.0) - e) / u)


def _split3(v):
    hi = v.astype(BF16)
    r1 = v - hi.astype(F32)
    mid = r1.astype(BF16)
    lo = (r1 - mid.astype(F32)).astype(BF16)
    return hi, mid, lo


def _ssd_chunk(r, xbc_ref, z_ref, gate_ref, cw_ref, cb_ref, gbias_ref, alog_ref, dskip_ref, nw_ref,
               sel_ref, shift_ref, y_ref, kbias_ref, cumt_ref,
               state_ref, hist_ref, cumcarry_ref, xdd_ref, yg_ref):
    L = SSD_CHUNK

    x_raw = xbc_ref[r]
    slot = lax.rem(pl.program_id(1), 2)
    hist = hist_ref[slot, r]
    hist_ref[1 - slot, r] = x_raw[L - _CONV_HIST:, :]
    x_ext = jnp.concatenate([hist, x_raw], axis=0)
    shifted = jnp.dot(shift_ref[...], x_ext, preferred_element_type=F32)
    conv = cb_ref[...] + cw_ref[SSD_CONV - 1:SSD_CONV, :] * x_raw.astype(F32)
    for j in range(1, SSD_CONV):
        k = SSD_CONV - 1 - j
        conv = conv + cw_ref[k:k + 1, :] * shifted[(j - 1) * L:j * L, :]
    xbc = conv * (1.0 / (1.0 + jnp.exp(-conv)))

    xs = xbc[:, :SSD_WIDTH]
    bm = xbc[:, SSD_WIDTH:SSD_WIDTH + SSD_BC_WIDTH]
    cm = xbc[:, SSD_WIDTH + SSD_BC_WIDTH:]

    lane_row = lax.broadcasted_iota(jnp.int32, (1, LANES), 1)
    is_dt = lane_row < SSD_HEADS
    is_gate = lane_row < SSD_HEADS + FOX_HEADS

    def f_lanes(v):
        return jnp.where(is_dt, 0.0, jnp.where(is_gate, v, 0.0))

    sp = _softplus(jnp.where(is_dt, 1.0, -1.0) * (gate_ref[r] + gbias_ref[...]))
    coef = jnp.where(is_dt, -jnp.exp(alog_ref[...]), f_lanes(-1.0))
    steps = sp * coef
    dt = sp

    rows = lax.broadcasted_iota(jnp.int32, (L, L), 0)
    cols = lax.broadcasted_iota(jnp.int32, (L, L), 1)
    tri = rows >= cols
    tri_b = jnp.where(tri, 1.0, 0.0).astype(BF16)
    sums = jnp.dot(tri_b, jnp.concatenate(_split3(steps), axis=1), preferred_element_type=F32)
    a_cs = sums[:, 0:LANES] + sums[:, LANES:2 * LANES] + sums[:, 2 * LANES:3 * LANES]
    cum = a_cs + cumcarry_ref[r, 0:1, :]
    cumcarry_ref[r, 0:1, :] = f_lanes(cum[L - 1:L, :])
    cumt_ref[r] = cum.T
    kb = jnp.concatenate(_split3(cum * (-LOG2E)), axis=1)
    kbias_ref[r] = jnp.dot(kb, sel_ref[...], preferred_element_type=F32).astype(BF16)

    a2 = a_cs * LOG2E
    a2t = a2.T

    def col(v, h):
        return jnp.broadcast_to(v[:, h:h + 1], (L, LANES))

    lane = lax.broadcasted_iota(jnp.int32, (L, PAIR), 1)
    lo_half = lane < HEAD_DIM
    heads_per_group = SSD_HEADS // SSD_GROUPS
    pairs_per_group = heads_per_group // 2

    cbs, cms, bms = [], [], []
    for g in range(SSD_GROUPS):
        cm_g = cm[:, g * SSD_STATE:(g + 1) * SSD_STATE]
        bm_g = bm[:, g * SSD_STATE:(g + 1) * SSD_STATE]
        cbs.append(lax.dot_general(cm_g.astype(BF16), bm_g.astype(BF16),
                                   (((1,), (1,)), ((), ())), preferred_element_type=F32))
        cms.append(cm_g)
        bms.append(bm_g)

    ssq = [None] * SSD_GROUPS
    cd_parts = []
    for p in range(SSD_HEADS // 2):
        g = p // pairs_per_group
        h0, h1 = 2 * p, 2 * p + 1
        sl = slice(p * PAIR, (p + 1) * PAIR)
        xs_p = xs[:, sl]
        acs = (col(a2, h0), col(a2, h1))
        a_pair = jnp.where(lo_half, acs[0], acs[1])
        a_end = a_pair[L - 1:L, :]
        xdt_p = xs_p * jnp.where(lo_half, col(dt, h0), col(dt, h1))
        xdd_ref[r, :, sl] = (xdt_p * jnp.exp2(a_end - a_pair)).astype(BF16)
        cd_parts.append(jnp.exp2(a_end))
        rhs = jnp.concatenate([xdt_p.astype(BF16), state_ref[r, :, sl].astype(BF16)], axis=0)
        ys = []
        for h, ac in zip((h0, h1), acs):
            decay = jnp.exp2(jnp.where(tri, ac - a2t[h:h + 1, :], NEG_BIG))
            lhs = jnp.concatenate([(cbs[g] * decay).astype(BF16),
                                   (cms[g] * jnp.exp2(ac)).astype(BF16)], axis=1)
            ys.append(jnp.dot(lhs, rhs, preferred_element_type=F32))
        y_p = jnp.where(lo_half, ys[0], ys[1]) + dskip_ref[:, sl] * xs_p
        zp = z_ref[r, :, sl].astype(F32)
        yg = y_p * (zp * (1.0 / (1.0 + jnp.exp(-zp))))
        yg_ref[r, :, sl] = yg
        s2 = jnp.sum(yg * yg, axis=-1, keepdims=True)
        ssq[g] = s2 if ssq[g] is None else ssq[g] + s2

    gw = SSD_WIDTH // SSD_GROUPS
    cd_row = jnp.concatenate(cd_parts, axis=1)
    for g in range(SSD_GROUPS):
        gs = slice(g * gw, (g + 1) * gw)
        scale = lax.rsqrt(ssq[g] * (1.0 / gw) + NORM_EPS)
        y_ref[r, :, gs] = (yg_ref[r, :, gs] * scale * nw_ref[:, gs]).astype(BF16)
        upd = jnp.dot(bms[g].T.astype(BF16), xdd_ref[r, :, gs], preferred_element_type=F32)
        state_ref[r, :, gs] = state_ref[r, :, gs] * cd_row[:, gs] + upd


def _ssd_body(xbc_ref, z_ref, gate_ref, cw_ref, cb_ref, gbias_ref, alog_ref, dskip_ref, nw_ref, sel_ref,
              shift_ref, y_ref, kbias_ref, cumt_ref,
              state_ref, hist_ref, cumcarry_ref, xdd_ref, yg_ref, *, rows_per_step):

    @pl.when(pl.program_id(1) == 0)
    def _():
        state_ref[...] = jnp.zeros_like(state_ref)
        hist_ref[...] = jnp.zeros_like(hist_ref)
        cumcarry_ref[...] = jnp.zeros_like(cumcarry_ref)

    for r in range(rows_per_step):
        _ssd_chunk(r, xbc_ref, z_ref, gate_ref, cw_ref, cb_ref, gbias_ref, alog_ref, dskip_ref, nw_ref,
                   sel_ref, shift_ref, y_ref, kbias_ref, cumt_ref,
                   state_ref, hist_ref, cumcarry_ref, xdd_ref, yg_ref)


_CONV_HIST = 16


def _conv_shift_matrix():
    import numpy as np
    L = SSD_CHUNK
    m = np.zeros(((SSD_CONV - 1) * L, _CONV_HIST + L), np.float32)
    for j in range(1, SSD_CONV):
        for t in range(L):
            m[(j - 1) * L + t, _CONV_HIST + t - j] = 1.0
    return jnp.asarray(m, BF16)


_KBIAS_TERMS = 3


def _kbias_selector():
    import numpy as np
    sel = np.zeros((_KBIAS_TERMS * LANES, FOX_WIDTH), np.float32)
    for h in range(FOX_HEADS):
        base = (h // 2) * PAIR + (HEAD_DIM if h % 2 == 0 else 0)
        for i in range(_KBIAS_TERMS):
            sel[i * LANES + SSD_HEADS + h, base + i] = 1.0
    return jnp.asarray(sel, BF16)


def _ssd_mixer(xbc, z, gate_raw, conv_w, conv_b, gate_bias, alog_p, dskip_e, norm_w):
    b, s, _ = xbc.shape
    L = SSD_CHUNK
    rows = 2 if b % 2 == 0 else 1
    sel = _kbias_selector()
    shift = _conv_shift_matrix()
    blk = lambda n: pl.BlockSpec((rows, L, n), lambda i, j: (i, j, 0))
    return pl.pallas_call(
        functools.partial(_ssd_body, rows_per_step=rows),
        grid=(b // rows, s // L),
        in_specs=[blk(SSD_XBC_WIDTH), blk(SSD_WIDTH), blk(LANES),
                  _const_spec(conv_w.shape), _const_spec(conv_b.shape), _const_spec(gate_bias.shape),
                  _const_spec(alog_p.shape), _const_spec(dskip_e.shape), _const_spec(norm_w.shape),
                  _const_spec(sel.shape), _const_spec(shift.shape)],
        out_specs=(blk(SSD_WIDTH), blk(FOX_WIDTH),
                   pl.BlockSpec((rows, LANES, L), lambda i, j: (i, 0, j))),
        out_shape=(jax.ShapeDtypeStruct((b, s, SSD_WIDTH), BF16),
                   jax.ShapeDtypeStruct((b, s, FOX_WIDTH), BF16),
                   jax.ShapeDtypeStruct((b, LANES, s), F32)),
        scratch_shapes=[
            pltpu.VMEM((rows, SSD_STATE, SSD_WIDTH), F32),
            pltpu.VMEM((2, rows, _CONV_HIST, SSD_XBC_WIDTH), BF16),
            pltpu.VMEM((rows, SUBLANES, LANES), F32),
            pltpu.VMEM((rows, L, SSD_WIDTH), BF16),
            pltpu.VMEM((rows, L, SSD_WIDTH), F32),
        ],
        compiler_params=pltpu.CompilerParams(
            dimension_semantics=("arbitrary", "arbitrary"), vmem_limit_bytes=_VMEM_LIMIT),
        name="ssd_mixer",
    )(xbc, z, gate_raw, conv_w, conv_b, gate_bias, alog_p, dskip_e, norm_w, sel, shift)


_VAUG_ROWS = HEAD_DIM + 16


def _fox_body(q_ref, k_ref, vt_ref, kbias_ref, cumt_ref, o_ref,
              kaug_ref, vaug_ref, m_ref, acc_ref, st_a, st_b, *, blk):
    s = q_ref.shape[1]
    nblk = s // blk
    lane = lax.broadcasted_iota(jnp.int32, (blk, PAIR), 1).astype(F32).astype(BF16)
    lo_half = lane < HEAD_DIM

    def build_keys(i, _):
        r0 = pl.multiple_of(i * blk, blk)
        kp = k_ref[0, pl.ds(r0, blk), :]
        kb = kbias_ref[0, pl.ds(r0, blk), :]
        kaug_ref[0, pl.ds(r0, blk), :] = jnp.where(lo_half, kp, kb)
        kaug_ref[1, pl.ds(r0, blk), :] = jnp.where(lo_half, kb, kp)
        for h in range(2):
            vaug_ref[h, 0:HEAD_DIM, pl.ds(r0, blk)] = vt_ref[h * HEAD_DIM:(h + 1) * HEAD_DIM,
                                                             pl.ds(r0, blk)]
            vaug_ref[h, HEAD_DIM:_VAUG_ROWS, pl.ds(r0, blk)] = jnp.ones(
                (_VAUG_ROWS - HEAD_DIM, blk), BF16)
        return 0

    lax.fori_loop(0, nblk, build_keys, 0)

    one = jnp.ones((blk, PAIR), BF16)
    zero = jnp.zeros((blk, PAIR), BF16)
    q_ones = (jnp.where(lane < HEAD_DIM + _KBIAS_TERMS, one, zero),
              jnp.where(lane < _KBIAS_TERMS, one, zero))
    key_idx = lax.broadcasted_iota(jnp.int32, (blk, blk), 0)
    qry_idx = lax.broadcasted_iota(jnp.int32, (blk, blk), 1)
    causal = key_idx <= qry_idx

    def queries(qi):
        q = q_ref[0, pl.ds(pl.multiple_of(qi * blk, blk), blk), :]
        return (jnp.where(lo_half, q, q_ones[0]), jnp.where(lo_half, q_ones[1], q))

    def scores(h, j, qa, st_ref):
        k0 = pl.multiple_of(j * blk, blk)
        st_ref[...] = lax.dot_general(kaug_ref[h, pl.ds(k0, blk), :], qa[h],
                                      (((1,), (1,)), ((), ())), preferred_element_type=F32)

    scores(0, 0, queries(0), st_a)

    def q_block(qi, _):
        q0 = pl.multiple_of(qi * blk, blk)
        qa = queries(qi)
        cq = cumt_ref[0, 0, :, pl.ds(q0, blk)] * LOG2E
        m_ref[...] = jnp.full(m_ref.shape, NEG_BIG, F32)
        acc_ref[...] = jnp.zeros_like(acc_ref)

        def softmax_pv(h, j, st_ref, masked):
            k0 = pl.multiple_of(j * blk, blk)
            st = st_ref[...]
            if masked:
                st = jnp.where(causal, st, NEG_BIG)
            cqh = cq[h:h + 1, :]
            m_prev = m_ref[h]
            m_new = jnp.maximum(m_prev, jnp.max(st, axis=0, keepdims=True) + cqh)
            pt = jnp.exp2(st - (m_new - cqh)).astype(BF16)
            alpha = jnp.exp2(m_prev - m_new)
            acc_ref[h] = alpha * acc_ref[h] + jnp.dot(
                vaug_ref[h, :, pl.ds(k0, blk)], pt, preferred_element_type=F32)
            m_ref[h] = m_new

        def off_diag(j):
            scores(1, j, qa, st_b)
            softmax_pv(0, j, st_a, False)
            scores(0, j + 1, qa, st_a)
            softmax_pv(1, j, st_b, False)

        def two_blocks(jj, _):
            off_diag(2 * jj)
            off_diag(2 * jj + 1)
            return 0

        lax.fori_loop(0, lax.shift_right_logical(qi, 1), two_blocks, 0)

        @pl.when(lax.bitwise_and(qi, 1) == 1)
        def _():
            off_diag(qi - 1)

        scores(1, qi, qa, st_b)
        softmax_pv(0, qi, st_a, True)
        qi_next = jnp.minimum(qi + 1, nblk - 1)
        scores(0, 0, queries(qi_next), st_a)
        softmax_pv(1, qi, st_b, True)

        out_t = jnp.concatenate(
            [acc_ref[h, 0:HEAD_DIM, :] * (1.0 / acc_ref[h, HEAD_DIM:HEAD_DIM + 1, :])
             for h in range(2)], axis=0)
        o_ref[0, pl.ds(q0, blk), :] = out_t.T.astype(BF16)
        return 0

    lax.fori_loop(0, nblk, q_block, 0)


def _fox_attention(q, k, vt, kbias, cumt_pairs, blk):
    b, s, _ = q.shape
    n_pairs = FOX_HEADS // 2
    pair_blk = pl.BlockSpec((1, s, PAIR), lambda i, j: (i, 0, j))
    return pl.pallas_call(
        functools.partial(_fox_body, blk=blk),
        grid=(b, n_pairs),
        in_specs=[pair_blk, pair_blk,
                  pl.BlockSpec((PAIR, s), lambda i, j: (j, i)),
                  pair_blk,
                  pl.BlockSpec((1, 1, 2, s), lambda i, j: (i, j, 0, 0))],
        out_specs=pair_blk,
        out_shape=jax.ShapeDtypeStruct((b, s, FOX_WIDTH), BF16),
        scratch_shapes=[
            pltpu.VMEM((2, s, PAIR), BF16),
            pltpu.VMEM((2, _VAUG_ROWS, s), BF16),
            pltpu.VMEM((2, 1, blk), F32),
            pltpu.VMEM((2, _VAUG_ROWS, blk), F32),
            pltpu.VMEM((blk, blk), F32),
            pltpu.VMEM((blk, blk), F32),
        ],
        compiler_params=pltpu.CompilerParams(
            dimension_semantics=("arbitrary", "arbitrary"), vmem_limit_bytes=_VMEM_LIMIT),
        name="fox_attention",
    )(q, k, vt, kbias, cumt_pairs)


def _ffn_body(x_ref, ys_ref, yf_ref, wo_ref, nw_ref, wup_ref, cw_ref, cb_ref, wdn_ref,
              o_ref, carry_ref, hg_ref, hv_ref, act_ref, *, tm, ch):
    @pl.when(pl.program_id(1) == 0)
    def _():
        carry_ref[...] = jnp.zeros_like(carry_ref)

    x1 = (x_ref[0]
          + jnp.dot(ys_ref[0], wo_ref[0:SSD_WIDTH, :], preferred_element_type=F32)
          + jnp.dot(yf_ref[0], wo_ref[SSD_WIDTH:, :], preferred_element_type=F32))
    o_ref[0] = x1
    ms = jnp.mean(x1 * x1, axis=-1, keepdims=True)
    hf = (x1 * lax.rsqrt(ms + NORM_EPS) * nw_ref[...]).astype(BF16)

    hist = SUBLANES

    def conv_half(h_ref, c0):
        cs = slice(c0, c0 + ch)
        h_ref[0:hist, :] = carry_ref[:, cs]
        h_ref[hist:hist + tm, :] = jnp.dot(hf, wup_ref[:, cs], preferred_element_type=F32)
        carry_ref[:, cs] = h_ref[tm:tm + hist, :]
        out = cb_ref[:, cs] + cw_ref[FFN_CONV - 1:FFN_CONV, cs] * h_ref[hist:hist + tm, :]
        for k in range(FFN_CONV - 1):
            off = hist - (FFN_CONV - 1) + k
            out = out + cw_ref[k:k + 1, cs] * h_ref[off:off + tm, :]
        return out

    for c0 in range(0, D_FF, ch):
        gate = conv_half(hg_ref, c0)
        val = conv_half(hv_ref, D_FF + c0)
        act_ref[:, c0:c0 + ch] = (gate * (1.0 / (1.0 + jnp.exp(-gate))) * val).astype(BF16)

    o_ref[0] = o_ref[0] + jnp.dot(act_ref[...], wdn_ref[...], preferred_element_type=F32)


def _out_ffn(x, y_ssd, y_fox, w_out, norm_w, w_up, conv_w, conv_b, w_down, tm, ch):
    b, s, _ = x.shape
    blk = lambda n: pl.BlockSpec((1, tm, n), lambda i, j: (i, j, 0))
    return pl.pallas_call(
        functools.partial(_ffn_body, tm=tm, ch=ch),
        grid=(b, s // tm),
        in_specs=[blk(D_MODEL), blk(SSD_WIDTH), blk(FOX_WIDTH),
                  _const_spec(w_out.shape), _const_spec(norm_w.shape), _const_spec(w_up.shape),
                  _const_spec(conv_w.shape), _const_spec(conv_b.shape), _const_spec(w_down.shape)],
        out_specs=blk(D_MODEL),
        out_shape=jax.ShapeDtypeStruct((b, s, D_MODEL), F32),
        scratch_shapes=[
            pltpu.VMEM((SUBLANES, 2 * D_FF), F32),
            pltpu.VMEM((tm + SUBLANES, ch), F32),
            pltpu.VMEM((tm + SUBLANES, ch), F32),
            pltpu.VMEM((tm, D_FF), BF16),
        ],
        compiler_params=pltpu.CompilerParams(
            dimension_semantics=("arbitrary", "arbitrary"), vmem_limit_bytes=_VMEM_LIMIT),
        name="out_ffn",
    )(x, y_ssd, y_fox, w_out, norm_w, w_up, conv_w, conv_b, w_down)


def _pad_lanes(v):
    return jnp.pad(v.astype(F32), (0, LANES - v.shape[0]))[None, :]


def _layer(x, norm_mix_w, w_in, ssd_conv_w, ssd_conv_b, ssd_dt_bias, ssd_a_log, ssd_d, ssd_norm_w,
           fox_f_bias, fox_q_norm_w, fox_k_norm_w, w_out, norm_ffn_w, w_up, ffn_conv_w, ffn_conv_b,
           w_down, *, tm_in, attn_blk, tm_ffn, ffn_ch):
    b, s, d = x.shape
    z_end = SSD_WIDTH
    xbc_end = z_end + SSD_XBC_WIDTH
    dt_end = xbc_end + SSD_HEADS
    q_end = dt_end + FOX_WIDTH
    k_end = q_end + FOX_WIDTH
    v_end = k_end + FOX_WIDTH

    def pad_cols(w):
        return jnp.pad(w, ((0, 0), (0, LANES - w.shape[1])))

    w_r = jnp.concatenate(
        [w_in[:, :xbc_end], w_in[:, dt_end:v_end],
         pad_cols(jnp.concatenate([w_in[:, xbc_end:dt_end], w_in[:, v_end:]], axis=1))],
        axis=1).astype(BF16)
    heads_per_tile = MXU_DIM // HEAD_DIM
    g = jnp.kron(jnp.eye(heads_per_tile, dtype=F32), jnp.ones((HEAD_DIM, HEAD_DIM), F32)).astype(BF16)
    qw = (jnp.tile(fox_q_norm_w.astype(F32), heads_per_tile) * (HEAD_DIM ** -0.5 * LOG2E))[None, :]
    kw = jnp.tile(fox_k_norm_w.astype(F32), heads_per_tile)[None, :]

    z, xbc, q, k, vt, gate_raw = _in_projection(
        x.reshape(b * s, d), norm_mix_w[None, :].astype(F32), w_r, g, qw, kw, tm_in)
    r3 = lambda a: a.reshape(b, s, a.shape[-1])

    y_ssd, kbias, cumt = _ssd_mixer(
        r3(xbc), r3(z), r3(gate_raw),
        ssd_conv_w.astype(F32), ssd_conv_b[None, :].astype(F32),
        _pad_lanes(jnp.concatenate([ssd_dt_bias, fox_f_bias])), _pad_lanes(ssd_a_log),
        jnp.repeat(ssd_d.astype(F32), HEAD_DIM)[None, :], ssd_norm_w[None, :].astype(F32))

    cumt_pairs = cumt[:, SSD_HEADS:SSD_HEADS + FOX_HEADS, :].reshape(b, FOX_HEADS // 2, 2, s)
    y_fox = _fox_attention(r3(q), r3(k), vt, kbias, cumt_pairs, attn_blk)

    return _out_ffn(x, y_ssd, y_fox, w_out.astype(BF16), norm_ffn_w[None, :].astype(F32),
                    w_up.astype(BF16), ffn_conv_w.astype(F32), ffn_conv_b[None, :].astype(F32),
                    w_down.astype(BF16), tm_ffn, ffn_ch)


def kernel(x, norm_mix_w, w_in, ssd_conv_w, ssd_conv_b, ssd_dt_bias, ssd_a_log, ssd_d, ssd_norm_w,
           fox_f_bias, fox_q_norm_w, fox_k_norm_w, w_out, norm_ffn_w, w_up, ffn_conv_w, ffn_conv_b,
           w_down):
    depth = w_in.shape[0]
    for layer in range(depth):
        x = _layer(x, norm_mix_w[layer], w_in[layer], ssd_conv_w[layer], ssd_conv_b[layer],
                   ssd_dt_bias[layer], ssd_a_log[layer], ssd_d[layer], ssd_norm_w[layer],
                   fox_f_bias[layer], fox_q_norm_w[layer], fox_k_norm_w[layer], w_out[layer],
                   norm_ffn_w[layer], w_up[layer], ffn_conv_w[layer], ffn_conv_b[layer],
                   w_down[layer], tm_in=512, attn_blk=512, tm_ffn=512, ffn_ch=256)
    return x
```

```python
import functools
import math

import jax
import jax.numpy as jnp
from jax import lax
from jax.experimental import pallas as pl
from jax.experimental.pallas import tpu as pltpu

F32 = jnp.float32
BF16 = jnp.bfloat16

D_MODEL = 1024
HEAD_DIM = 64
SSD_HEADS = 16
SSD_GROUPS = 2
SSD_STATE = 128
SSD_CONV = 4
SSD_CHUNK = 128
SSD_WIDTH = SSD_HEADS * HEAD_DIM
SSD_BC_WIDTH = SSD_GROUPS * SSD_STATE
SSD_XBC_WIDTH = SSD_WIDTH + 2 * SSD_BC_WIDTH
FOX_HEADS = 16
FOX_WIDTH = FOX_HEADS * HEAD_DIM
D_FF = 2816
FFN_CONV = 3
NORM_EPS = 1e-6
LOG2E = math.log2(math.e)

LANES = 128
SUBLANES = 8
MXU_DIM = 256
PAIR = 2 * HEAD_DIM
NEG_BIG = -1e30

_Z0 = 0
_XBC0 = _Z0 + SSD_WIDTH
_Q0 = _XBC0 + SSD_XBC_WIDTH
_K0 = _Q0 + FOX_WIDTH
_V0 = _K0 + FOX_WIDTH
_GATE0 = _V0 + FOX_WIDTH
_IN_COLS_PADDED = _GATE0 + LANES

_VMEM_LIMIT = 56 * 1024 * 1024


def _const_spec(shape):
    zeros = (0,) * len(shape)
    return pl.BlockSpec(shape, lambda *_: zeros, pipeline_mode=pl.Buffered(1))


def _inproj_body(x_ref, nw_ref, w_ref, g_ref, qw_ref, kw_ref,
                 z_ref, xbc_ref, q_ref, k_ref, vt_ref, gate_ref, vtmp_ref):
    x = x_ref[...]
    ms = jnp.mean(x * x, axis=-1, keepdims=True)
    h = (x * lax.rsqrt(ms + NORM_EPS) * nw_ref[...]).astype(BF16)

    def proj(c0, n):
        return jnp.dot(h, w_ref[:, c0:c0 + n], preferred_element_type=F32)

    z_ref[...] = proj(_Z0, SSD_WIDTH).astype(BF16)
    xbc_ref[...] = proj(_XBC0, SSD_XBC_WIDTH).astype(BF16)

    vtmp_ref[...] = proj(_V0, FOX_WIDTH)
    for c in range(0, FOX_WIDTH, MXU_DIM):
        vt_ref[c:c + MXU_DIM, :] = vtmp_ref[:, c:c + MXU_DIM].T.astype(BF16)

    g = g_ref[...]
    for base, o_ref, hw_ref in ((_Q0, q_ref, qw_ref), (_K0, k_ref, kw_ref)):
        vtmp_ref[...] = proj(base, FOX_WIDTH)
        for c in range(0, FOX_WIDTH, MXU_DIM):
            t = vtmp_ref[:, c:c + MXU_DIM]
            ss = jnp.dot((t * t).astype(BF16), g, preferred_element_type=F32)
            o_ref[:, c:c + MXU_DIM] = (
                t * lax.rsqrt(ss * (1.0 / HEAD_DIM) + NORM_EPS) * hw_ref[...]).astype(BF16)

    gate_ref[...] = proj(_GATE0, LANES)


def _in_projection(x2, norm_w, w_r, g, qw, kw, tm):
    t = x2.shape[0]
    row = lambda n: pl.BlockSpec((tm, n), lambda i: (i, 0))
    out_shape = (
        jax.ShapeDtypeStruct((t, SSD_WIDTH), BF16),
        jax.ShapeDtypeStruct((t, SSD_XBC_WIDTH), BF16),
        jax.ShapeDtypeStruct((t, FOX_WIDTH), BF16),
        jax.ShapeDtypeStruct((t, FOX_WIDTH), BF16),
        jax.ShapeDtypeStruct((FOX_WIDTH, t), BF16),
        jax.ShapeDtypeStruct((t, LANES), F32),
    )
    return pl.pallas_call(
        _inproj_body,
        grid=(t // tm,),
        in_specs=[row(D_MODEL), _const_spec((1, D_MODEL)), _const_spec(w_r.shape),
                  _const_spec(g.shape), _const_spec(qw.shape), _const_spec(kw.shape)],
        out_specs=(row(SSD_WIDTH), row(SSD_XBC_WIDTH), row(FOX_WIDTH), row(FOX_WIDTH),
                   pl.BlockSpec((FOX_WIDTH, tm), lambda i: (0, i)), row(LANES)),
        out_shape=out_shape,
        scratch_shapes=[pltpu.VMEM((tm, FOX_WIDTH), F32)],
        compiler_params=pltpu.CompilerParams(
            dimension_semantics=("arbitrary",), vmem_limit_bytes=_VMEM_LIMIT),
        name="in_projection",
    )(x2, norm_w, w_r, g, qw, kw)


def _softplus(x):
    e = jnp.exp(-jnp.abs(x))
    u = 1.0 + e
    return jnp.maximum(x, 0.0) + (jnp.log(u) - ((u - 1.0) - e) / u)


def _split3(v):
    hi = v.astype(BF16)
    r1 = v - hi.astype(F32)
    mid = r1.astype(BF16)
    lo = (r1 - mid.astype(F32)).astype(BF16)
    return hi, mid, lo


def _ssd_chunk(r, xbc_ref, z_ref, gate_ref, cw_ref, cb_ref, gbias_ref, alog_ref, dskip_ref, nw_ref,
               sel_ref, shift_ref, y_ref, kbias_ref, cumt_ref,
               state_ref, hist_ref, cumcarry_ref, xdd_ref, yg_ref):
    L = SSD_CHUNK

    x_raw = xbc_ref[r]
    slot = lax.rem(pl.program_id(1), 2)
    hist = hist_ref[slot, r]
    hist_ref[1 - slot, r] = x_raw[L - _CONV_HIST:, :]
    x_ext = jnp.concatenate([hist, x_raw], axis=0)
    shifted = jnp.dot(shift_ref[...], x_ext, preferred_element_type=F32)
    conv = cb_ref[...] + cw_ref[SSD_CONV - 1:SSD_CONV, :] * x_raw.astype(F32)
    for j in range(1, SSD_CONV):
        k = SSD_CONV - 1 - j
        conv = conv + cw_ref[k:k + 1, :] * shifted[(j - 1) * L:j * L, :]
    xbc = conv * (1.0 / (1.0 + jnp.exp(-conv)))

    xs = xbc[:, :SSD_WIDTH]
    bm = xbc[:, SSD_WIDTH:SSD_WIDTH + SSD_BC_WIDTH]
    cm = xbc[:, SSD_WIDTH + SSD_BC_WIDTH:]

    lane_row = lax.broadcasted_iota(jnp.int32, (1, LANES), 1)
    is_dt = lane_row < SSD_HEADS
    is_gate = lane_row < SSD_HEADS + FOX_HEADS

    def f_lanes(v):
        return jnp.where(is_dt, 0.0, jnp.where(is_gate, v, 0.0))

    sp = _softplus(jnp.where(is_dt, 1.0, -1.0) * (gate_ref[r] + gbias_ref[...]))
    coef = jnp.where(is_dt, -jnp.exp(alog_ref[...]), f_lanes(-1.0))
    steps = sp * coef
    dt = sp

    rows = lax.broadcasted_iota(jnp.int32, (L, L), 0)
    cols = lax.broadcasted_iota(jnp.int32, (L, L), 1)
    tri = rows >= cols
    tri_b = jnp.where(tri, 1.0, 0.0).astype(BF16)
    sums = jnp.dot(tri_b, jnp.concatenate(_split3(steps), axis=1), preferred_element_type=F32)
    a_cs = sums[:, 0:LANES] + sums[:, LANES:2 * LANES] + sums[:, 2 * LANES:3 * LANES]
    cum = a_cs + cumcarry_ref[r, 0:1, :]
    cumcarry_ref[r, 0:1, :] = f_lanes(cum[L - 1:L, :])
    cumt_ref[r] = cum.T
    kb = jnp.concatenate(_split3(cum * (-LOG2E)), axis=1)
    kbias_ref[r] = jnp.dot(kb, sel_ref[...], preferred_element_type=F32).astype(BF16)

    a2 = a_cs * LOG2E
    a2t = a2.T

    def col(v, h):
        return jnp.broadcast_to(v[:, h:h + 1], (L, LANES))

    lane = lax.broadcasted_iota(jnp.int32, (L, PAIR), 1)
    lo_half = lane < HEAD_DIM
    heads_per_group = SSD_HEADS // SSD_GROUPS
    pairs_per_group = heads_per_group // 2

    cbs, cms, bms = [], [], []
    for g in range(SSD_GROUPS):
        cm_g = cm[:, g * SSD_STATE:(g + 1) * SSD_STATE]
        bm_g = bm[:, g * SSD_STATE:(g + 1) * SSD_STATE]
        cbs.append(lax.dot_general(cm_g.astype(BF16), bm_g.astype(BF16),
                                   (((1,), (1,)), ((), ())), preferred_element_type=F32))
        cms.append(cm_g)
        bms.append(bm_g)

    ssq = [None] * SSD_GROUPS
    cd_parts = []
    for p in range(SSD_HEADS // 2):
        g = p // pairs_per_group
        h0, h1 = 2 * p, 2 * p + 1
        sl = slice(p * PAIR, (p + 1) * PAIR)
        xs_p = xs[:, sl]
        acs = (col(a2, h0), col(a2, h1))
        a_pair = jnp.where(lo_half, acs[0], acs[1])
        a_end = a_pair[L - 1:L, :]
        xdt_p = xs_p * jnp.where(lo_half, col(dt, h0), col(dt, h1))
        xdd_ref[r, :, sl] = (xdt_p * jnp.exp2(a_end - a_pair)).astype(BF16)
        cd_parts.append(jnp.exp2(a_end))
        rhs = jnp.concatenate([xdt_p.astype(BF16), state_ref[r, :, sl].astype(BF16)], axis=0)
        ys = []
        for h, ac in zip((h0, h1), acs):
            decay = jnp.exp2(jnp.where(tri, ac - a2t[h:h + 1, :], NEG_BIG))
            lhs = jnp.concatenate([(cbs[g] * decay).astype(BF16),
                                   (cms[g] * jnp.exp2(ac)).astype(BF16)], axis=1)
            ys.append(jnp.dot(lhs, rhs, preferred_element_type=F32))
        y_p = jnp.where(lo_half, ys[0], ys[1]) + dskip_ref[:, sl] * xs_p
        zp = z_ref[r, :, sl].astype(F32)
        yg = y_p * (zp * (1.0 / (1.0 + jnp.exp(-zp))))
        yg_ref[r, :, sl] = yg
        s2 = jnp.sum(yg * yg, axis=-1, keepdims=True)
        ssq[g] = s2 if ssq[g] is None else ssq[g] + s2

    gw = SSD_WIDTH // SSD_GROUPS
    cd_row = jnp.concatenate(cd_parts, axis=1)
    for g in range(SSD_GROUPS):
        gs = slice(g * gw, (g + 1) * gw)
        scale = lax.rsqrt(ssq[g] * (1.0 / gw) + NORM_EPS)
        y_ref[r, :, gs] = (yg_ref[r, :, gs] * scale * nw_ref[:, gs]).astype(BF16)
        upd = jnp.dot(bms[g].T.astype(BF16), xdd_ref[r, :, gs], preferred_element_type=F32)
        state_ref[r, :, gs] = state_ref[r, :, gs] * cd_row[:, gs] + upd


def _ssd_body(xbc_ref, z_ref, gate_ref, cw_ref, cb_ref, gbias_ref, alog_ref, dskip_ref, nw_ref, sel_ref,
              shift_ref, y_ref, kbias_ref, cumt_ref,
              state_ref, hist_ref, cumcarry_ref, xdd_ref, yg_ref, *, rows_per_step):

    @pl.when(pl.program_id(1) == 0)
    def _():
        state_ref[...] = jnp.zeros_like(state_ref)
        hist_ref[...] = jnp.zeros_like(hist_ref)
        cumcarry_ref[...] = jnp.zeros_like(cumcarry_ref)

    for r in range(rows_per_step):
        _ssd_chunk(r, xbc_ref, z_ref, gate_ref, cw_ref, cb_ref, gbias_ref, alog_ref, dskip_ref, nw_ref,
                   sel_ref, shift_ref, y_ref, kbias_ref, cumt_ref,
                   state_ref, hist_ref, cumcarry_ref, xdd_ref, yg_ref)


_CONV_HIST = 16


def _conv_shift_matrix():
    import numpy as np
    L = SSD_CHUNK
    m = np.zeros(((SSD_CONV - 1) * L, _CONV_HIST + L), np.float32)
    for j in range(1, SSD_CONV):
        for t in range(L):
            m[(j - 1) * L + t, _CONV_HIST + t - j] = 1.0
    return jnp.asarray(m, BF16)


_KBIAS_TERMS = 3


def _kbias_selector():
    import numpy as np
    sel = np.zeros((_KBIAS_TERMS * LANES, FOX_WIDTH), np.float32)
    for h in range(FOX_HEADS):
        base = (h // 2) * PAIR + (HEAD_DIM if h % 2 == 0 else 0)
        for i in range(_KBIAS_TERMS):
            sel[i * LANES + SSD_HEADS + h, base + i] = 1.0
    return jnp.asarray(sel, BF16)


def _ssd_mixer(xbc, z, gate_raw, conv_w, conv_b, gate_bias, alog_p, dskip_e, norm_w):
    b, s, _ = xbc.shape
    L = SSD_CHUNK
    rows = 2 if b % 2 == 0 else 1
    sel = _kbias_selector()
    shift = _conv_shift_matrix()
    blk = lambda n: pl.BlockSpec((rows, L, n), lambda i, j: (i, j, 0))
    return pl.pallas_call(
        functools.partial(_ssd_body, rows_per_step=rows),
        grid=(b // rows, s // L),
        in_specs=[blk(SSD_XBC_WIDTH), blk(SSD_WIDTH), blk(LANES),
                  _const_spec(conv_w.shape), _const_spec(conv_b.shape), _const_spec(gate_bias.shape),
                  _const_spec(alog_p.shape), _const_spec(dskip_e.shape), _const_spec(norm_w.shape),
                  _const_spec(sel.shape), _const_spec(shift.shape)],
        out_specs=(blk(SSD_WIDTH), blk(FOX_WIDTH),
                   pl.BlockSpec((rows, LANES, L), lambda i, j: (i, 0, j))),
        out_shape=(jax.ShapeDtypeStruct((b, s, SSD_WIDTH), BF16),
                   jax.ShapeDtypeStruct((b, s, FOX_WIDTH), BF16),
                   jax.ShapeDtypeStruct((b, LANES, s), F32)),
        scratch_shapes=[
            pltpu.VMEM((rows, SSD_STATE, SSD_WIDTH), F32),
            pltpu.VMEM((2, rows, _CONV_HIST, SSD_XBC_WIDTH), BF16),
            pltpu.VMEM((rows, SUBLANES, LANES), F32),
            pltpu.VMEM((rows, L, SSD_WIDTH), BF16),
            pltpu.VMEM((rows, L, SSD_WIDTH), F32),
        ],
        compiler_params=pltpu.CompilerParams(
            dimension_semantics=("arbitrary", "arbitrary"), vmem_limit_bytes=_VMEM_LIMIT),
        name="ssd_mixer",
    )(xbc, z, gate_raw, conv_w, conv_b, gate_bias, alog_p, dskip_e, norm_w, sel, shift)


_VAUG_ROWS = HEAD_DIM + 16
_KV_UNROLL = 4


def _fox_body(q_ref, k_ref, vt_ref, kbias_ref, cumt_ref, o_ref,
              kaug_ref, vaug_ref, m_ref, acc_ref, st_a, st_b, *, blk):
    s = q_ref.shape[1]
    nblk = s // blk
    lane = lax.broadcasted_iota(jnp.int32, (blk, PAIR), 1).astype(F32).astype(BF16)
    lo_half = lane < HEAD_DIM

    def build_keys(i, _):
        r0 = pl.multiple_of(i * blk, blk)
        kp = k_ref[0, pl.ds(r0, blk), :]
        kb = kbias_ref[0, pl.ds(r0, blk), :]
        kaug_ref[0, pl.ds(r0, blk), :] = jnp.where(lo_half, kp, kb)
        kaug_ref[1, pl.ds(r0, blk), :] = jnp.where(lo_half, kb, kp)
        for h in range(2):
            vaug_ref[h, 0:HEAD_DIM, pl.ds(r0, blk)] = vt_ref[h * HEAD_DIM:(h + 1) * HEAD_DIM,
                                                             pl.ds(r0, blk)]
            vaug_ref[h, HEAD_DIM:_VAUG_ROWS, pl.ds(r0, blk)] = jnp.ones(
                (_VAUG_ROWS - HEAD_DIM, blk), BF16)
        return 0

    lax.fori_loop(0, nblk, build_keys, 0)

    one = jnp.ones((blk, PAIR), BF16)
    zero = jnp.zeros((blk, PAIR), BF16)
    q_ones = (jnp.where(lane < HEAD_DIM + _KBIAS_TERMS, one, zero),
              jnp.where(lane < _KBIAS_TERMS, one, zero))
    key_idx = lax.broadcasted_iota(jnp.int32, (blk, blk), 0)
    qry_idx = lax.broadcasted_iota(jnp.int32, (blk, blk), 1)
    causal = key_idx <= qry_idx

    def queries(qi):
        q = q_ref[0, pl.ds(pl.multiple_of(qi * blk, blk), blk), :]
        return (jnp.where(lo_half, q, q_ones[0]), jnp.where(lo_half, q_ones[1], q))

    def scores(h, j, qa, st_ref):
        k0 = pl.multiple_of(j * blk, blk)
        st_ref[...] = lax.dot_general(kaug_ref[h, pl.ds(k0, blk), :], qa[h],
                                      (((1,), (1,)), ((), ())), preferred_element_type=F32)

    scores(0, 0, queries(0), st_a)

    def q_block(qi, _):
        q0 = pl.multiple_of(qi * blk, blk)
        qa = queries(qi)
        cq = cumt_ref[0, 0, :, pl.ds(q0, blk)] * LOG2E
        m_ref[...] = jnp.full(m_ref.shape, NEG_BIG, F32)
        acc_ref[...] = jnp.zeros_like(acc_ref)

        def softmax_pv(h, k0, st, q_lo, masked):
            nk, nq = st.shape
            qs = slice(q_lo, q_lo + nq)
            if masked:
                st = jnp.where(causal[:nk, :nq], st, NEG_BIG)
            cqh = cq[h:h + 1, qs]
            m_prev = m_ref[h, :, qs]
            m_new = jnp.maximum(m_prev, jnp.max(st, axis=0, keepdims=True) + cqh)
            pt = jnp.exp2(st - (m_new - cqh)).astype(BF16)
            alpha = jnp.exp2(m_prev - m_new)
            acc_ref[h, :, qs] = alpha * acc_ref[h, :, qs] + jnp.dot(
                vaug_ref[h, :, pl.ds(pl.multiple_of(k0, nk), nk)], pt, preferred_element_type=F32)
            m_ref[h, :, qs] = m_new

        def off_diag(j):
            scores(1, j, qa, st_b)
            softmax_pv(0, j * blk, st_a[...], 0, False)
            scores(0, j + 1, qa, st_a)
            softmax_pv(1, j * blk, st_b[...], 0, False)

        def diagonal(h, st_ref):
            half = blk // 2
            softmax_pv(h, q0, st_ref[0:half, :], 0, True)
            softmax_pv(h, q0 + half, st_ref[half:, half:], half, True)

        def run_blocks(j0, n):
            for t in range(n):
                off_diag(j0 + t)

        def unrolled(jj, _):
            run_blocks(_KV_UNROLL * jj, _KV_UNROLL)
            return 0

        lax.fori_loop(0, qi // _KV_UNROLL, unrolled, 0)
        done = qi - lax.rem(qi, _KV_UNROLL)
        n = _KV_UNROLL // 2
        while n >= 1:
            pl.when(lax.bitwise_and(qi, n) != 0)(functools.partial(run_blocks, done, n))
            done = done + lax.bitwise_and(qi, n)
            n //= 2

        scores(1, qi, qa, st_b)
        diagonal(0, st_a)
        qi_next = jnp.minimum(qi + 1, nblk - 1)
        scores(0, 0, queries(qi_next), st_a)
        diagonal(1, st_b)

        out_t = jnp.concatenate(
            [acc_ref[h, 0:HEAD_DIM, :] * (1.0 / acc_ref[h, HEAD_DIM:HEAD_DIM + 1, :])
             for h in range(2)], axis=0)
        o_ref[0, pl.ds(q0, blk), :] = out_t.T.astype(BF16)
        return 0

    lax.fori_loop(0, nblk, q_block, 0)


def _fox_attention(q, k, vt, kbias, cumt_pairs, blk):
    b, s, _ = q.shape
    n_pairs = FOX_HEADS // 2
    pair_blk = pl.BlockSpec((1, s, PAIR), lambda i, j: (i, 0, j))
    return pl.pallas_call(
        functools.partial(_fox_body, blk=blk),
        grid=(b, n_pairs),
        in_specs=[pair_blk, pair_blk,
                  pl.BlockSpec((PAIR, s), lambda i, j: (j, i)),
                  pair_blk,
                  pl.BlockSpec((1, 1, 2, s), lambda i, j: (i, j, 0, 0))],
        out_specs=pair_blk,
        out_shape=jax.ShapeDtypeStruct((b, s, FOX_WIDTH), BF16),
        scratch_shapes=[
            pltpu.VMEM((2, s, PAIR), BF16),
            pltpu.VMEM((2, _VAUG_ROWS, s), BF16),
            pltpu.VMEM((2, 1, blk), F32),
            pltpu.VMEM((2, _VAUG_ROWS, blk), F32),
            pltpu.VMEM((blk, blk), F32),
            pltpu.VMEM((blk, blk), F32),
        ],
        compiler_params=pltpu.CompilerParams(
            dimension_semantics=("arbitrary", "arbitrary"), vmem_limit_bytes=_VMEM_LIMIT),
        name="fox_attention",
    )(q, k, vt, kbias, cumt_pairs)


def _ffn_body(x_ref, ys_ref, yf_ref, wo_ref, nw_ref, wup_ref, cw_ref, cb_ref, wdn_ref,
              o_ref, carry_ref, hg_ref, hv_ref, act_ref, *, tm, ch):
    @pl.when(pl.program_id(1) == 0)
    def _():
        carry_ref[...] = jnp.zeros_like(carry_ref)

    x1 = (x_ref[0]
          + jnp.dot(ys_ref[0], wo_ref[0:SSD_WIDTH, :], preferred_element_type=F32)
          + jnp.dot(yf_ref[0], wo_ref[SSD_WIDTH:, :], preferred_element_type=F32))
    o_ref[0] = x1
    ms = jnp.mean(x1 * x1, axis=-1, keepdims=True)
    hf = (x1 * lax.rsqrt(ms + NORM_EPS) * nw_ref[...]).astype(BF16)

    hist = SUBLANES

    def conv_half(h_ref, c0):
        cs = slice(c0, c0 + ch)
        h_ref[0:hist, :] = carry_ref[:, cs]
        h_ref[hist:hist + tm, :] = jnp.dot(hf, wup_ref[:, cs], preferred_element_type=F32)
        carry_ref[:, cs] = h_ref[tm:tm + hist, :]
        out = cb_ref[:, cs] + cw_ref[FFN_CONV - 1:FFN_CONV, cs] * h_ref[hist:hist + tm, :]
        for k in range(FFN_CONV - 1):
            off = hist - (FFN_CONV - 1) + k
            out = out + cw_ref[k:k + 1, cs] * h_ref[off:off + tm, :]
        return out

    for c0 in range(0, D_FF, ch):
        gate = conv_half(hg_ref, c0)
        val = conv_half(hv_ref, D_FF + c0)
        act_ref[:, c0:c0 + ch] = (gate * (1.0 / (1.0 + jnp.exp(-gate))) * val).astype(BF16)

    o_ref[0] = o_ref[0] + jnp.dot(act_ref[...], wdn_ref[...], preferred_element_type=F32)


def _out_ffn(x, y_ssd, y_fox, w_out, norm_w, w_up, conv_w, conv_b, w_down, tm, ch):
    b, s, _ = x.shape
    blk = lambda n: pl.BlockSpec((1, tm, n), lambda i, j: (i, j, 0))
    return pl.pallas_call(
        functools.partial(_ffn_body, tm=tm, ch=ch),
        grid=(b, s // tm),
        in_specs=[blk(D_MODEL), blk(SSD_WIDTH), blk(FOX_WIDTH),
                  _const_spec(w_out.shape), _const_spec(norm_w.shape), _const_spec(w_up.shape),
                  _const_spec(conv_w.shape), _const_spec(conv_b.shape), _const_spec(w_down.shape)],
        out_specs=blk(D_MODEL),
        out_shape=jax.ShapeDtypeStruct((b, s, D_MODEL), F32),
        scratch_shapes=[
            pltpu.VMEM((SUBLANES, 2 * D_FF), F32),
            pltpu.VMEM((tm + SUBLANES, ch), F32),
            pltpu.VMEM((tm + SUBLANES, ch), F32),
            pltpu.VMEM((tm, D_FF), BF16),
        ],
        compiler_params=pltpu.CompilerParams(
            dimension_semantics=("arbitrary", "arbitrary"), vmem_limit_bytes=_VMEM_LIMIT),
        name="out_ffn",
    )(x, y_ssd, y_fox, w_out, norm_w, w_up, conv_w, conv_b, w_down)


def _pad_lanes(v):
    return jnp.pad(v.astype(F32), (0, LANES - v.shape[0]))[None, :]


def _layer(x, norm_mix_w, w_in, ssd_conv_w, ssd_conv_b, ssd_dt_bias, ssd_a_log, ssd_d, ssd_norm_w,
           fox_f_bias, fox_q_norm_w, fox_k_norm_w, w_out, norm_ffn_w, w_up, ffn_conv_w, ffn_conv_b,
           w_down, *, tm_in, attn_blk, tm_ffn, ffn_ch):
    b, s, d = x.shape
    z_end = SSD_WIDTH
    xbc_end = z_end + SSD_XBC_WIDTH
    dt_end = xbc_end + SSD_HEADS
    q_end = dt_end + FOX_WIDTH
    k_end = q_end + FOX_WIDTH
    v_end = k_end + FOX_WIDTH

    def pad_cols(w):
        return jnp.pad(w, ((0, 0), (0, LANES - w.shape[1])))

    w_r = jnp.concatenate(
        [w_in[:, :xbc_end], w_in[:, dt_end:v_end],
         pad_cols(jnp.concatenate([w_in[:, xbc_end:dt_end], w_in[:, v_end:]], axis=1))],
        axis=1).astype(BF16)
    heads_per_tile = MXU_DIM // HEAD_DIM
    g = jnp.kron(jnp.eye(heads_per_tile, dtype=F32), jnp.ones((HEAD_DIM, HEAD_DIM), F32)).astype(BF16)
    qw = (jnp.tile(fox_q_norm_w.astype(F32), heads_per_tile) * (HEAD_DIM ** -0.5 * LOG2E))[None, :]
    kw = jnp.tile(fox_k_norm_w.astype(F32), heads_per_tile)[None, :]

    z, xbc, q, k, vt, gate_raw = _in_projection(
        x.reshape(b * s, d), norm_mix_w[None, :].astype(F32), w_r, g, qw, kw, tm_in)
    r3 = lambda a: a.reshape(b, s, a.shape[-1])

    y_ssd, kbias, cumt = _ssd_mixer(
        r3(xbc), r3(z), r3(gate_raw),
        ssd_conv_w.astype(F32), ssd_conv_b[None, :].astype(F32),
        _pad_lanes(jnp.concatenate([ssd_dt_bias, fox_f_bias])), _pad_lanes(ssd_a_log),
        jnp.repeat(ssd_d.astype(F32), HEAD_DIM)[None, :], ssd_norm_w[None, :].astype(F32))

    cumt_pairs = cumt[:, SSD_HEADS:SSD_HEADS + FOX_HEADS, :].reshape(b, FOX_HEADS // 2, 2, s)
    y_fox = _fox_attention(r3(q), r3(k), vt, kbias, cumt_pairs, attn_blk)

    return _out_ffn(x, y_ssd, y_fox, w_out.astype(BF16), norm_ffn_w[None, :].astype(F32),
                    w_up.astype(BF16), ffn_conv_w.astype(F32), ffn_conv_b[None, :].astype(F32),
                    w_down.astype(BF16), tm_ffn, ffn_ch)


def kernel(x, norm_mix_w, w_in, ssd_conv_w, ssd_conv_b, ssd_dt_bias, ssd_a_log, ssd_d, ssd_norm_w,
           fox_f_bias, fox_q_norm_w, fox_k_norm_w, w_out, norm_ffn_w, w_up, ffn_conv_w, ffn_conv_b,
           w_down):
    depth = w_in.shape[0]
    for layer in range(depth):
        x = _layer(x, norm_mix_w[layer], w_in[layer], ssd_conv_w[layer], ssd_conv_b[layer],
                   ssd_dt_bias[layer], ssd_a_log[layer], ssd_d[layer], ssd_norm_w[layer],
                   fox_f_bias[layer], fox_q_norm_w[layer], fox_k_norm_w[layer], w_out[layer],
                   norm_ffn_w[layer], w_up[layer], ffn_conv_w[layer], ffn_conv_b[layer],
                   w_down[layer], tm_in=512, attn_blk=512, tm_ffn=512, ffn_ch=256)
    return x
```

```python
import functools
import math

import jax
import jax.numpy as jnp
from jax import lax
from jax.experimental import pallas as pl
from jax.experimental.pallas import tpu as pltpu

F32 = jnp.float32
BF16 = jnp.bfloat16

D_MODEL = 1024
HEAD_DIM = 64
SSD_HEADS = 16
SSD_GROUPS = 2
SSD_STATE = 128
SSD_CONV = 4
SSD_CHUNK = 128
SSD_WIDTH = SSD_HEADS * HEAD_DIM
SSD_BC_WIDTH = SSD_GROUPS * SSD_STATE
SSD_XBC_WIDTH = SSD_WIDTH + 2 * SSD_BC_WIDTH
FOX_HEADS = 16
FOX_WIDTH = FOX_HEADS * HEAD_DIM
D_FF = 2816
FFN_CONV = 3
NORM_EPS = 1e-6
LOG2E = math.log2(math.e)

LANES = 128
SUBLANES = 8
MXU_DIM = 256
PAIR = 2 * HEAD_DIM
NEG_BIG = -1e30

_Z0 = 0
_XBC0 = _Z0 + SSD_WIDTH
_Q0 = _XBC0 + SSD_XBC_WIDTH
_K0 = _Q0 + FOX_WIDTH
_V0 = _K0 + FOX_WIDTH
_GATE0 = _V0 + FOX_WIDTH
_IN_COLS_PADDED = _GATE0 + LANES

_VMEM_LIMIT = 56 * 1024 * 1024


def _const_spec(shape):
    zeros = (0,) * len(shape)
    return pl.BlockSpec(shape, lambda *_: zeros, pipeline_mode=pl.Buffered(1))


def _softplus(x):
    e = jnp.exp(-jnp.abs(x))
    u = 1.0 + e
    return jnp.maximum(x, 0.0) + (jnp.log(u) - ((u - 1.0) - e) / u)


def _split3(v):
    hi = v.astype(BF16)
    r1 = v - hi.astype(F32)
    mid = r1.astype(BF16)
    lo = (r1 - mid.astype(F32)).astype(BF16)
    return hi, mid, lo


def _ssd_chunk(r, xbc_ref, z_ref, gate_ref, cw_ref, cb_ref, gbias_ref, alog_ref, dskip_ref, nw_ref,
               sel_ref, shift_ref, y_ref, kbias_ref, cumt_ref,
               state_ref, hist_ref, cumcarry_ref, xdd_ref, yg_ref):
    L = SSD_CHUNK

    x_raw = xbc_ref[r]
    slot = lax.rem(pl.program_id(1), 2)
    hist = hist_ref[slot, r]
    hist_ref[1 - slot, r] = x_raw[L - _CONV_HIST:, :]
    x_ext = jnp.concatenate([hist, x_raw], axis=0)
    shifted = jnp.dot(shift_ref[...], x_ext, preferred_element_type=F32)
    conv = cb_ref[...] + cw_ref[SSD_CONV - 1:SSD_CONV, :] * x_raw.astype(F32)
    for j in range(1, SSD_CONV):
        k = SSD_CONV - 1 - j
        conv = conv + cw_ref[k:k + 1, :] * shifted[(j - 1) * L:j * L, :]
    xbc = conv * (1.0 / (1.0 + jnp.exp(-conv)))
    yield

    xs = xbc[:, :SSD_WIDTH]
    bm = xbc[:, SSD_WIDTH:SSD_WIDTH + SSD_BC_WIDTH]
    cm = xbc[:, SSD_WIDTH + SSD_BC_WIDTH:]

    lane_row = lax.broadcasted_iota(jnp.int32, (1, LANES), 1)
    is_dt = lane_row < SSD_HEADS
    is_gate = lane_row < SSD_HEADS + FOX_HEADS

    def f_lanes(v):
        return jnp.where(is_dt, 0.0, jnp.where(is_gate, v, 0.0))

    sp = _softplus(jnp.where(is_dt, 1.0, -1.0) * (gate_ref[r] + gbias_ref[...]))
    coef = jnp.where(is_dt, -jnp.exp(alog_ref[...]), f_lanes(-1.0))
    steps = sp * coef
    dt = sp

    rows = lax.broadcasted_iota(jnp.int32, (L, L), 0)
    cols = lax.broadcasted_iota(jnp.int32, (L, L), 1)
    tri = rows >= cols
    tri_b = jnp.where(tri, 1.0, 0.0).astype(BF16)
    sums = jnp.dot(tri_b, jnp.concatenate(_split3(steps), axis=1), preferred_element_type=F32)
    a_cs = sums[:, 0:LANES] + sums[:, LANES:2 * LANES] + sums[:, 2 * LANES:3 * LANES]
    cum = a_cs + cumcarry_ref[r, 0:1, :]
    cumcarry_ref[r, 0:1, :] = f_lanes(cum[L - 1:L, :])
    cumt_ref[r] = cum.T
    kb = jnp.concatenate(_split3(cum * (-LOG2E)), axis=1)
    kbias_ref[r] = jnp.dot(kb, sel_ref[...], preferred_element_type=F32).astype(BF16)

    a2 = a_cs * LOG2E
    a2t = a2.T

    def col(v, h):
        return jnp.broadcast_to(v[:, h:h + 1], (L, LANES))

    lane = lax.broadcasted_iota(jnp.int32, (L, PAIR), 1)
    lo_half = lane < HEAD_DIM
    heads_per_group = SSD_HEADS // SSD_GROUPS
    pairs_per_group = heads_per_group // 2

    cbs, cms, bms = [], [], []
    for g in range(SSD_GROUPS):
        cm_g = cm[:, g * SSD_STATE:(g + 1) * SSD_STATE]
        bm_g = bm[:, g * SSD_STATE:(g + 1) * SSD_STATE]
        cbs.append(lax.dot_general(cm_g.astype(BF16), bm_g.astype(BF16),
                                   (((1,), (1,)), ((), ())), preferred_element_type=F32))
        cms.append(cm_g)
        bms.append(bm_g)
    yield

    ssq = [None] * SSD_GROUPS
    cd_parts = []
    for p in range(SSD_HEADS // 2):
        g = p // pairs_per_group
        h0, h1 = 2 * p, 2 * p + 1
        sl = slice(p * PAIR, (p + 1) * PAIR)
        xs_p = xs[:, sl]
        acs = (col(a2, h0), col(a2, h1))
        a_pair = jnp.where(lo_half, acs[0], acs[1])
        a_end = a_pair[L - 1:L, :]
        xdt_p = xs_p * jnp.where(lo_half, col(dt, h0), col(dt, h1))
        xdd_ref[r, :, sl] = (xdt_p * jnp.exp2(a_end - a_pair)).astype(BF16)
        cd_parts.append(jnp.exp2(a_end))
        rhs = jnp.concatenate([xdt_p.astype(BF16), state_ref[r, :, sl].astype(BF16)], axis=0)
        ys = []
        for h, ac in zip((h0, h1), acs):
            decay = jnp.exp2(jnp.where(tri, ac - a2t[h:h + 1, :], NEG_BIG))
            lhs = jnp.concatenate([(cbs[g] * decay).astype(BF16),
                                   (cms[g] * jnp.exp2(ac)).astype(BF16)], axis=1)
            ys.append(jnp.dot(lhs, rhs, preferred_element_type=F32))
        y_p = jnp.where(lo_half, ys[0], ys[1]) + dskip_ref[:, sl] * xs_p
        zp = z_ref[r, :, sl].astype(F32)
        yg = y_p * (zp * (1.0 / (1.0 + jnp.exp(-zp))))
        yg_ref[r, :, sl] = yg
        s2 = jnp.sum(yg * yg, axis=-1, keepdims=True)
        ssq[g] = s2 if ssq[g] is None else ssq[g] + s2
        if p % 2 == 1:
            yield

    gw = SSD_WIDTH // SSD_GROUPS
    cd_row = jnp.concatenate(cd_parts, axis=1)
    for g in range(SSD_GROUPS):
        gs = slice(g * gw, (g + 1) * gw)
        scale = lax.rsqrt(ssq[g] * (1.0 / gw) + NORM_EPS)
        y_ref[r, :, gs] = (yg_ref[r, :, gs] * scale * nw_ref[:, gs]).astype(BF16)
        upd = jnp.dot(bms[g].T.astype(BF16), xdd_ref[r, :, gs], preferred_element_type=F32)
        state_ref[r, :, gs] = state_ref[r, :, gs] * cd_row[:, gs] + upd


def _mix_in_body(x_ref, mw_ref, w_ref, g_ref, qw_ref, kw_ref,
                 cw_ref, cb_ref, gbias_ref, alog_ref, dskip_ref, nw_ref, sel_ref, shift_ref,
                 q_ref, k_ref, vt_ref, y_ref, kbias_ref, cumt_ref,
                 state_ref, hist_ref, cumcarry_ref, xdd_ref, yg_ref,
                 xbc_s, z_s, gate_s, ptmp_ref, *, rows_per_step):
    L = SSD_CHUNK
    R = rows_per_step

    @pl.when(pl.program_id(1) == 0)
    def _():
        state_ref[...] = jnp.zeros_like(state_ref)
        hist_ref[...] = jnp.zeros_like(hist_ref)
        cumcarry_ref[...] = jnp.zeros_like(cumcarry_ref)

    x = x_ref[...].reshape(R * L, D_MODEL)
    ms = jnp.mean(x * x, axis=-1, keepdims=True)
    h = (x * lax.rsqrt(ms + NORM_EPS) * mw_ref[...]).astype(BF16)

    def proj(c0, n):
        return jnp.dot(h, w_ref[:, c0:c0 + n], preferred_element_type=F32)

    z_s[...] = proj(_Z0, SSD_WIDTH).astype(BF16).reshape(R, L, SSD_WIDTH)
    xbc_s[...] = proj(_XBC0, SSD_XBC_WIDTH).astype(BF16).reshape(R, L, SSD_XBC_WIDTH)
    gate_s[...] = proj(_GATE0, LANES).reshape(R, L, LANES)

    def normed_heads(base, o_ref, hw_ref):
        ptmp_ref[...] = proj(base, FOX_WIDTH)
        g = g_ref[...]
        for c in range(0, FOX_WIDTH, MXU_DIM):
            t = ptmp_ref[:, c:c + MXU_DIM]
            ss = jnp.dot((t * t).astype(BF16), g, preferred_element_type=F32)
            o_ref[:, :, c:c + MXU_DIM] = (
                t * lax.rsqrt(ss * (1.0 / HEAD_DIM) + NORM_EPS) * hw_ref[...]
            ).astype(BF16).reshape(R, L, MXU_DIM)

    def values_transposed():
        ptmp_ref[...] = proj(_V0, FOX_WIDTH)
        for c in range(0, FOX_WIDTH, MXU_DIM):
            for r in range(R):
                vt_ref[r, c:c + MXU_DIM, :] = ptmp_ref[r * L:(r + 1) * L, c:c + MXU_DIM].T.astype(BF16)

    projections = [functools.partial(normed_heads, _Q0, q_ref, qw_ref),
                   functools.partial(normed_heads, _K0, k_ref, kw_ref),
                   values_transposed]
    chunks = [_ssd_chunk(r, xbc_s, z_s, gate_s, cw_ref, cb_ref, gbias_ref, alog_ref, dskip_ref,
                         nw_ref, sel_ref, shift_ref, y_ref, kbias_ref, cumt_ref,
                         state_ref, hist_ref, cumcarry_ref, xdd_ref, yg_ref) for r in range(R)]
    stage = 0
    while chunks:
        chunks = [c for c in chunks if next(c, _DONE) is not _DONE]
        if 1 <= stage <= len(projections):
            projections[stage - 1]()
        stage += 1


_DONE = object()


_CONV_HIST = 16


def _conv_shift_matrix():
    import numpy as np
    L = SSD_CHUNK
    m = np.zeros(((SSD_CONV - 1) * L, _CONV_HIST + L), np.float32)
    for j in range(1, SSD_CONV):
        for t in range(L):
            m[(j - 1) * L + t, _CONV_HIST + t - j] = 1.0
    return jnp.asarray(m, BF16)


_KBIAS_TERMS = 3


def _kbias_selector():
    import numpy as np
    sel = np.zeros((_KBIAS_TERMS * LANES, FOX_WIDTH), np.float32)
    for h in range(FOX_HEADS):
        base = (h // 2) * PAIR + (HEAD_DIM if h % 2 == 0 else 0)
        for i in range(_KBIAS_TERMS):
            sel[i * LANES + SSD_HEADS + h, base + i] = 1.0
    return jnp.asarray(sel, BF16)


def _mix_in(x, norm_w, w_r, g, qw, kw, conv_w, conv_b, gate_bias, alog_p, dskip_e, ssd_norm_w):
    b, s, _ = x.shape
    L = SSD_CHUNK
    rows = 2 if b % 2 == 0 else 1
    sel = _kbias_selector()
    shift = _conv_shift_matrix()
    blk = lambda n: pl.BlockSpec((rows, L, n), lambda i, j: (i, j, 0))
    consts = (norm_w, w_r, g, qw, kw, conv_w, conv_b, gate_bias, alog_p, dskip_e, ssd_norm_w, sel, shift)
    return pl.pallas_call(
        functools.partial(_mix_in_body, rows_per_step=rows),
        grid=(b // rows, s // L),
        in_specs=[blk(D_MODEL)] + [_const_spec(c.shape) for c in consts],
        out_specs=(blk(FOX_WIDTH), blk(FOX_WIDTH),
                   pl.BlockSpec((rows, FOX_WIDTH, L), lambda i, j: (i, 0, j)),
                   blk(SSD_WIDTH), blk(FOX_WIDTH),
                   pl.BlockSpec((rows, LANES, L), lambda i, j: (i, 0, j))),
        out_shape=(jax.ShapeDtypeStruct((b, s, FOX_WIDTH), BF16),
                   jax.ShapeDtypeStruct((b, s, FOX_WIDTH), BF16),
                   jax.ShapeDtypeStruct((b, FOX_WIDTH, s), BF16),
                   jax.ShapeDtypeStruct((b, s, SSD_WIDTH), BF16),
                   jax.ShapeDtypeStruct((b, s, FOX_WIDTH), BF16),
                   jax.ShapeDtypeStruct((b, LANES, s), F32)),
        scratch_shapes=[
            pltpu.VMEM((rows, SSD_STATE, SSD_WIDTH), F32),
            pltpu.VMEM((2, rows, _CONV_HIST, SSD_XBC_WIDTH), BF16),
            pltpu.VMEM((rows, SUBLANES, LANES), F32),
            pltpu.VMEM((rows, L, SSD_WIDTH), BF16),
            pltpu.VMEM((rows, L, SSD_WIDTH), F32),
            pltpu.VMEM((rows, L, SSD_XBC_WIDTH), BF16),
            pltpu.VMEM((rows, L, SSD_WIDTH), BF16),
            pltpu.VMEM((rows, L, LANES), F32),
            pltpu.VMEM((rows * L, FOX_WIDTH), F32),
        ],
        compiler_params=pltpu.CompilerParams(
            dimension_semantics=("arbitrary", "arbitrary"), vmem_limit_bytes=_VMEM_LIMIT),
        name="mix_in",
    )(x, *consts)


_VAUG_ROWS = HEAD_DIM + 16
_KV_UNROLL = 4


def _fox_body(q_ref, k_ref, vt_ref, kbias_ref, cumt_ref, o_ref,
              kaug_ref, vaug_ref, m_ref, acc_ref, st_a, st_b, *, blk):
    s = q_ref.shape[1]
    nblk = s // blk
    lane = lax.broadcasted_iota(jnp.int32, (blk, PAIR), 1).astype(F32).astype(BF16)
    lo_half = lane < HEAD_DIM

    def build_keys(i, _):
        r0 = pl.multiple_of(i * blk, blk)
        kp = k_ref[0, pl.ds(r0, blk), :]
        kb = kbias_ref[0, pl.ds(r0, blk), :]
        kaug_ref[0, pl.ds(r0, blk), :] = jnp.where(lo_half, kp, kb)
        kaug_ref[1, pl.ds(r0, blk), :] = jnp.where(lo_half, kb, kp)
        for h in range(2):
            vaug_ref[h, 0:HEAD_DIM, pl.ds(r0, blk)] = vt_ref[0, h * HEAD_DIM:(h + 1) * HEAD_DIM,
                                                             pl.ds(r0, blk)]
            vaug_ref[h, HEAD_DIM:_VAUG_ROWS, pl.ds(r0, blk)] = jnp.ones(
                (_VAUG_ROWS - HEAD_DIM, blk), BF16)
        return 0

    lax.fori_loop(0, nblk, build_keys, 0)

    one = jnp.ones((blk, PAIR), BF16)
    zero = jnp.zeros((blk, PAIR), BF16)
    q_ones = (jnp.where(lane < HEAD_DIM + _KBIAS_TERMS, one, zero),
              jnp.where(lane < _KBIAS_TERMS, one, zero))
    key_idx = lax.broadcasted_iota(jnp.int32, (blk, blk), 0)
    qry_idx = lax.broadcasted_iota(jnp.int32, (blk, blk), 1)
    causal = key_idx <= qry_idx

    def queries(qi):
        q = q_ref[0, pl.ds(pl.multiple_of(qi * blk, blk), blk), :]
        return (jnp.where(lo_half, q, q_ones[0]), jnp.where(lo_half, q_ones[1], q))

    def scores(h, j, qa, st_ref):
        k0 = pl.multiple_of(j * blk, blk)
        st_ref[...] = lax.dot_general(kaug_ref[h, pl.ds(k0, blk), :], qa[h],
                                      (((1,), (1,)), ((), ())), preferred_element_type=F32)

    scores(0, 0, queries(0), st_a)

    def q_block(qi, _):
        q0 = pl.multiple_of(qi * blk, blk)
        qa = queries(qi)
        cq = cumt_ref[0, 0, :, pl.ds(q0, blk)] * LOG2E
        m_ref[...] = jnp.full(m_ref.shape, NEG_BIG, F32)
        acc_ref[...] = jnp.zeros_like(acc_ref)

        def softmax_pv(h, k0, st, q_lo, masked):
            nk, nq = st.shape
            qs = slice(q_lo, q_lo + nq)
            if masked:
                st = jnp.where(causal[:nk, :nq], st, NEG_BIG)
            cqh = cq[h:h + 1, qs]
            m_prev = m_ref[h, :, qs]
            m_new = jnp.maximum(m_prev, jnp.max(st, axis=0, keepdims=True) + cqh)
            pt = jnp.exp2(st - (m_new - cqh)).astype(BF16)
            alpha = jnp.exp2(m_prev - m_new)
            acc_ref[h, :, qs] = alpha * acc_ref[h, :, qs] + jnp.dot(
                vaug_ref[h, :, pl.ds(pl.multiple_of(k0, nk), nk)], pt, preferred_element_type=F32)
            m_ref[h, :, qs] = m_new

        def off_diag(j):
            scores(1, j, qa, st_b)
            softmax_pv(0, j * blk, st_a[...], 0, False)
            scores(0, j + 1, qa, st_a)
            softmax_pv(1, j * blk, st_b[...], 0, False)

        def diagonal(h, st_ref):
            half = blk // 2
            softmax_pv(h, q0, st_ref[0:half, :], 0, True)
            softmax_pv(h, q0 + half, st_ref[half:, half:], half, True)

        def run_blocks(j0, n):
            for t in range(n):
                off_diag(j0 + t)

        def unrolled(jj, _):
            run_blocks(_KV_UNROLL * jj, _KV_UNROLL)
            return 0

        lax.fori_loop(0, qi // _KV_UNROLL, unrolled, 0)
        done = qi - lax.rem(qi, _KV_UNROLL)
        n = _KV_UNROLL // 2
        while n >= 1:
            pl.when(lax.bitwise_and(qi, n) != 0)(functools.partial(run_blocks, done, n))
            done = done + lax.bitwise_and(qi, n)
            n //= 2

        scores(1, qi, qa, st_b)
        diagonal(0, st_a)
        qi_next = jnp.minimum(qi + 1, nblk - 1)
        scores(0, 0, queries(qi_next), st_a)
        diagonal(1, st_b)

        out_t = jnp.concatenate(
            [acc_ref[h, 0:HEAD_DIM, :] * (1.0 / acc_ref[h, HEAD_DIM:HEAD_DIM + 1, :])
             for h in range(2)], axis=0)
        o_ref[0, pl.ds(q0, blk), :] = out_t.T.astype(BF16)
        return 0

    lax.fori_loop(0, nblk, q_block, 0)


def _fox_attention(q, k, vt, kbias, cumt_pairs, blk):
    b, s, _ = q.shape
    n_pairs = FOX_HEADS // 2
    pair_blk = pl.BlockSpec((1, s, PAIR), lambda i, j: (i, 0, j))
    return pl.pallas_call(
        functools.partial(_fox_body, blk=blk),
        grid=(b, n_pairs),
        in_specs=[pair_blk, pair_blk,
                  pl.BlockSpec((1, PAIR, s), lambda i, j: (i, j, 0)),
                  pair_blk,
                  pl.BlockSpec((1, 1, 2, s), lambda i, j: (i, j, 0, 0))],
        out_specs=pair_blk,
        out_shape=jax.ShapeDtypeStruct((b, s, FOX_WIDTH), BF16),
        scratch_shapes=[
            pltpu.VMEM((2, s, PAIR), BF16),
            pltpu.VMEM((2, _VAUG_ROWS, s), BF16),
            pltpu.VMEM((2, 1, blk), F32),
            pltpu.VMEM((2, _VAUG_ROWS, blk), F32),
            pltpu.VMEM((blk, blk), F32),
            pltpu.VMEM((blk, blk), F32),
        ],
        compiler_params=pltpu.CompilerParams(
            dimension_semantics=("arbitrary", "arbitrary"), vmem_limit_bytes=_VMEM_LIMIT),
        name="fox_attention",
    )(q, k, vt, kbias, cumt_pairs)


def _ffn_body(x_ref, ys_ref, yf_ref, wo_ref, nw_ref, wup_ref, cw_ref, cb_ref, wdn_ref,
              o_ref, carry_ref, hg_ref, hv_ref, act_ref, *, tm, ch):
    @pl.when(pl.program_id(1) == 0)
    def _():
        carry_ref[...] = jnp.zeros_like(carry_ref)

    x1 = (x_ref[0]
          + jnp.dot(ys_ref[0], wo_ref[0:SSD_WIDTH, :], preferred_element_type=F32)
          + jnp.dot(yf_ref[0], wo_ref[SSD_WIDTH:, :], preferred_element_type=F32))
    o_ref[0] = x1
    ms = jnp.mean(x1 * x1, axis=-1, keepdims=True)
    hf = (x1 * lax.rsqrt(ms + NORM_EPS) * nw_ref[...]).astype(BF16)

    hist = SUBLANES

    def conv_half(h_ref, c0):
        cs = slice(c0, c0 + ch)
        h_ref[0:hist, :] = carry_ref[:, cs]
        h_ref[hist:hist + tm, :] = jnp.dot(hf, wup_ref[:, cs], preferred_element_type=F32)
        carry_ref[:, cs] = h_ref[tm:tm + hist, :]
        out = cb_ref[:, cs] + cw_ref[FFN_CONV - 1:FFN_CONV, cs] * h_ref[hist:hist + tm, :]
        for k in range(FFN_CONV - 1):
            off = hist - (FFN_CONV - 1) + k
            out = out + cw_ref[k:k + 1, cs] * h_ref[off:off + tm, :]
        return out

    for c0 in range(0, D_FF, ch):
        gate = conv_half(hg_ref, c0)
        val = conv_half(hv_ref, D_FF + c0)
        act_ref[:, c0:c0 + ch] = (gate * (1.0 / (1.0 + jnp.exp(-gate))) * val).astype(BF16)

    o_ref[0] = o_ref[0] + jnp.dot(act_ref[...], wdn_ref[...], preferred_element_type=F32)


def _out_ffn(x, y_ssd, y_fox, w_out, norm_w, w_up, conv_w, conv_b, w_down, tm, ch):
    b, s, _ = x.shape
    blk = lambda n: pl.BlockSpec((1, tm, n), lambda i, j: (i, j, 0))
    return pl.pallas_call(
        functools.partial(_ffn_body, tm=tm, ch=ch),
        grid=(b, s // tm),
        in_specs=[blk(D_MODEL), blk(SSD_WIDTH), blk(FOX_WIDTH),
                  _const_spec(w_out.shape), _const_spec(norm_w.shape), _const_spec(w_up.shape),
                  _const_spec(conv_w.shape), _const_spec(conv_b.shape), _const_spec(w_down.shape)],
        out_specs=blk(D_MODEL),
        out_shape=jax.ShapeDtypeStruct((b, s, D_MODEL), F32),
        scratch_shapes=[
            pltpu.VMEM((SUBLANES, 2 * D_FF), F32),
            pltpu.VMEM((tm + SUBLANES, ch), F32),
            pltpu.VMEM((tm + SUBLANES, ch), F32),
            pltpu.VMEM((tm, D_FF), BF16),
        ],
        compiler_params=pltpu.CompilerParams(
            dimension_semantics=("arbitrary", "arbitrary"), vmem_limit_bytes=_VMEM_LIMIT),
        name="out_ffn",
    )(x, y_ssd, y_fox, w_out, norm_w, w_up, conv_w, conv_b, w_down)


def _pad_lanes(v):
    return jnp.pad(v.astype(F32), (0, LANES - v.shape[0]))[None, :]


def _layer(x, norm_mix_w, w_in, ssd_conv_w, ssd_conv_b, ssd_dt_bias, ssd_a_log, ssd_d, ssd_norm_w,
           fox_f_bias, fox_q_norm_w, fox_k_norm_w, w_out, norm_ffn_w, w_up, ffn_conv_w, ffn_conv_b,
           w_down, *, attn_blk, tm_ffn, ffn_ch):
    b, s, d = x.shape
    z_end = SSD_WIDTH
    xbc_end = z_end + SSD_XBC_WIDTH
    dt_end = xbc_end + SSD_HEADS
    q_end = dt_end + FOX_WIDTH
    k_end = q_end + FOX_WIDTH
    v_end = k_end + FOX_WIDTH

    def pad_cols(w):
        return jnp.pad(w, ((0, 0), (0, LANES - w.shape[1])))

    w_r = jnp.concatenate(
        [w_in[:, :xbc_end], w_in[:, dt_end:v_end],
         pad_cols(jnp.concatenate([w_in[:, xbc_end:dt_end], w_in[:, v_end:]], axis=1))],
        axis=1).astype(BF16)
    heads_per_tile = MXU_DIM // HEAD_DIM
    g = jnp.kron(jnp.eye(heads_per_tile, dtype=F32), jnp.ones((HEAD_DIM, HEAD_DIM), F32)).astype(BF16)
    qw = (jnp.tile(fox_q_norm_w.astype(F32), heads_per_tile) * (HEAD_DIM ** -0.5 * LOG2E))[None, :]
    kw = jnp.tile(fox_k_norm_w.astype(F32), heads_per_tile)[None, :]

    q, k, vt, y_ssd, kbias, cumt = _mix_in(
        x, norm_mix_w[None, :].astype(F32), w_r, g, qw, kw,
        ssd_conv_w.astype(F32), ssd_conv_b[None, :].astype(F32),
        _pad_lanes(jnp.concatenate([ssd_dt_bias, fox_f_bias])), _pad_lanes(ssd_a_log),
        jnp.repeat(ssd_d.astype(F32), HEAD_DIM)[None, :], ssd_norm_w[None, :].astype(F32))

    cumt_pairs = cumt[:, SSD_HEADS:SSD_HEADS + FOX_HEADS, :].reshape(b, FOX_HEADS // 2, 2, s)
    y_fox = _fox_attention(q, k, vt, kbias, cumt_pairs, attn_blk)

    return _out_ffn(x, y_ssd, y_fox, w_out.astype(BF16), norm_ffn_w[None, :].astype(F32),
                    w_up.astype(BF16), ffn_conv_w.astype(F32), ffn_conv_b[None, :].astype(F32),
                    w_down.astype(BF16), tm_ffn, ffn_ch)


def kernel(x, norm_mix_w, w_in, ssd_conv_w, ssd_conv_b, ssd_dt_bias, ssd_a_log, ssd_d, ssd_norm_w,
           fox_f_bias, fox_q_norm_w, fox_k_norm_w, w_out, norm_ffn_w, w_up, ffn_conv_w, ffn_conv_b,
           w_down):
    depth = w_in.shape[0]
    for layer in range(depth):
        x = _layer(x, norm_mix_w[layer], w_in[layer], ssd_conv_w[layer], ssd_conv_b[layer],
                   ssd_dt_bias[layer], ssd_a_log[layer], ssd_d[layer], ssd_norm_w[layer],
                   fox_f_bias[layer], fox_q_norm_w[layer], fox_k_norm_w[layer], w_out[layer],
                   norm_ffn_w[layer], w_up[layer], ffn_conv_w[layer], ffn_conv_b[layer],
                   w_down[layer], attn_blk=512, tm_ffn=512, ffn_ch=256)
    return x
```

```python
import functools
import math

import jax
import jax.numpy as jnp
from jax import lax
from jax.experimental import pallas as pl
from jax.experimental.pallas import tpu as pltpu

F32 = jnp.float32
BF16 = jnp.bfloat16

D_MODEL = 1024
HEAD_DIM = 64
SSD_HEADS = 16
SSD_GROUPS = 2
SSD_STATE = 128
SSD_CONV = 4
SSD_CHUNK = 128
SSD_WIDTH = SSD_HEADS * HEAD_DIM
SSD_BC_WIDTH = SSD_GROUPS * SSD_STATE
SSD_XBC_WIDTH = SSD_WIDTH + 2 * SSD_BC_WIDTH
FOX_HEADS = 16
FOX_WIDTH = FOX_HEADS * HEAD_DIM
D_FF = 2816
FFN_CONV = 3
NORM_EPS = 1e-6
LOG2E = math.log2(math.e)

LANES = 128
SUBLANES = 8
MXU_DIM = 256
PAIR = 2 * HEAD_DIM
NEG_BIG = -1e30

_Z0 = 0
_XBC0 = _Z0 + SSD_WIDTH
_Q0 = _XBC0 + SSD_XBC_WIDTH
_K0 = _Q0 + FOX_WIDTH
_V0 = _K0 + FOX_WIDTH
_GATE0 = _V0 + FOX_WIDTH
_IN_COLS_PADDED = _GATE0 + LANES

_VMEM_LIMIT = 56 * 1024 * 1024


def _const_spec(shape):
    zeros = (0,) * len(shape)
    return pl.BlockSpec(shape, lambda *_: zeros, pipeline_mode=pl.Buffered(1))


def _softplus(x):
    e = jnp.exp(-jnp.abs(x))
    u = 1.0 + e
    return jnp.maximum(x, 0.0) + (jnp.log(u) - ((u - 1.0) - e) / u)


def _split3(v):
    hi = v.astype(BF16)
    r1 = v - hi.astype(F32)
    mid = r1.astype(BF16)
    lo = (r1 - mid.astype(F32)).astype(BF16)
    return hi, mid, lo


def _ssd_chunk(r, xbc_ref, z_ref, gate_ref, cw_ref, cb_ref, gbias_ref, alog_ref, dskip_ref, nw_ref,
               sel_ref, shift_ref, y_ref, kbias_ref, cumt_ref,
               state_ref, hist_ref, cumcarry_ref, xdd_ref, yg_ref):
    L = SSD_CHUNK

    x_raw = xbc_ref[r]
    slot = lax.rem(pl.program_id(1), 2)
    hist = hist_ref[slot, r]
    hist_ref[1 - slot, r] = x_raw[L - _CONV_HIST:, :]
    x_ext = jnp.concatenate([hist, x_raw], axis=0)
    shifted = jnp.dot(shift_ref[...], x_ext, preferred_element_type=F32)
    conv = cb_ref[...] + cw_ref[SSD_CONV - 1:SSD_CONV, :] * x_raw.astype(F32)
    for j in range(1, SSD_CONV):
        k = SSD_CONV - 1 - j
        conv = conv + cw_ref[k:k + 1, :] * shifted[(j - 1) * L:j * L, :]
    xbc = conv * (1.0 / (1.0 + jnp.exp(-conv)))
    yield

    xs = xbc[:, :SSD_WIDTH]
    bm = xbc[:, SSD_WIDTH:SSD_WIDTH + SSD_BC_WIDTH]
    cm = xbc[:, SSD_WIDTH + SSD_BC_WIDTH:]

    lane_row = lax.broadcasted_iota(jnp.int32, (1, LANES), 1)
    is_dt = lane_row < SSD_HEADS
    is_gate = lane_row < SSD_HEADS + FOX_HEADS

    def f_lanes(v):
        return jnp.where(is_dt, 0.0, jnp.where(is_gate, v, 0.0))

    sp = _softplus(jnp.where(is_dt, 1.0, -1.0) * (gate_ref[r] + gbias_ref[...]))
    coef = jnp.where(is_dt, -jnp.exp(alog_ref[...]), f_lanes(-1.0))
    steps = sp * coef
    dt = sp

    rows = lax.broadcasted_iota(jnp.int32, (L, L), 0)
    cols = lax.broadcasted_iota(jnp.int32, (L, L), 1)
    tri = rows >= cols
    tri_b = jnp.where(tri, 1.0, 0.0).astype(BF16)
    sums = jnp.dot(tri_b, jnp.concatenate(_split3(steps), axis=1), preferred_element_type=F32)
    a_cs = sums[:, 0:LANES] + sums[:, LANES:2 * LANES] + sums[:, 2 * LANES:3 * LANES]
    cum = a_cs + cumcarry_ref[r, 0:1, :]
    cumcarry_ref[r, 0:1, :] = f_lanes(cum[L - 1:L, :])
    cumt_ref[r] = cum.T[SSD_HEADS:SSD_HEADS + FOX_HEADS, :]
    kb = jnp.concatenate(_split3(cum * (-LOG2E)), axis=1)
    kbias_ref[r] = jnp.dot(kb, sel_ref[...], preferred_element_type=F32).astype(BF16)

    a2 = a_cs * LOG2E
    a2t = a2.T

    def col(v, h):
        return jnp.broadcast_to(v[:, h:h + 1], (L, LANES))

    lane = lax.broadcasted_iota(jnp.int32, (L, PAIR), 1)
    lo_half = lane < HEAD_DIM
    heads_per_group = SSD_HEADS // SSD_GROUPS
    pairs_per_group = heads_per_group // 2

    cbs, cms, bms = [], [], []
    for g in range(SSD_GROUPS):
        cm_g = cm[:, g * SSD_STATE:(g + 1) * SSD_STATE]
        bm_g = bm[:, g * SSD_STATE:(g + 1) * SSD_STATE]
        cbs.append(lax.dot_general(cm_g.astype(BF16), bm_g.astype(BF16),
                                   (((1,), (1,)), ((), ())), preferred_element_type=F32))
        cms.append(cm_g)
        bms.append(bm_g)
    yield

    ssq = [None] * SSD_GROUPS
    cd_parts = []
    for p in range(SSD_HEADS // 2):
        g = p // pairs_per_group
        h0, h1 = 2 * p, 2 * p + 1
        sl = slice(p * PAIR, (p + 1) * PAIR)
        xs_p = xs[:, sl]
        acs = (col(a2, h0), col(a2, h1))
        a_pair = jnp.where(lo_half, acs[0], acs[1])
        a_end = a_pair[L - 1:L, :]
        xdt_p = xs_p * jnp.where(lo_half, col(dt, h0), col(dt, h1))
        xdd_ref[r, :, sl] = (xdt_p * jnp.exp2(a_end - a_pair)).astype(BF16)
        cd_parts.append(jnp.exp2(a_end))
        rhs = jnp.concatenate([xdt_p.astype(BF16), state_ref[r, :, sl].astype(BF16)], axis=0)
        ys = []
        for h, ac in zip((h0, h1), acs):
            decay = jnp.exp2(jnp.where(tri, ac - a2t[h:h + 1, :], NEG_BIG))
            lhs = jnp.concatenate([(cbs[g] * decay).astype(BF16),
                                   (cms[g] * jnp.exp2(ac)).astype(BF16)], axis=1)
            ys.append(jnp.dot(lhs, rhs, preferred_element_type=F32))
        y_p = jnp.where(lo_half, ys[0], ys[1]) + dskip_ref[:, sl] * xs_p
        zp = z_ref[r, :, sl].astype(F32)
        yg = y_p * (zp * (1.0 / (1.0 + jnp.exp(-zp))))
        yg_ref[r, :, sl] = yg
        s2 = jnp.sum(yg * yg, axis=-1, keepdims=True)
        ssq[g] = s2 if ssq[g] is None else ssq[g] + s2
        if p % 2 == 1:
            yield

    gw = SSD_WIDTH // SSD_GROUPS
    cd_row = jnp.concatenate(cd_parts, axis=1)
    for g in range(SSD_GROUPS):
        gs = slice(g * gw, (g + 1) * gw)
        scale = lax.rsqrt(ssq[g] * (1.0 / gw) + NORM_EPS)
        y_ref[r, :, gs] = (yg_ref[r, :, gs] * scale * nw_ref[:, gs]).astype(BF16)
        upd = jnp.dot(bms[g].T.astype(BF16), xdd_ref[r, :, gs], preferred_element_type=F32)
        state_ref[r, :, gs] = state_ref[r, :, gs] * cd_row[:, gs] + upd


def _mix_in_body(x_ref, mw_ref, w_ref, g_ref, qw_ref, kw_ref,
                 cw_ref, cb_ref, gbias_ref, alog_ref, dskip_ref, nw_ref, sel_ref, shift_ref,
                 q_ref, k_ref, vt_ref, y_ref, kbias_ref, cumt_ref,
                 state_ref, hist_ref, cumcarry_ref, xdd_ref, yg_ref,
                 xbc_s, z_s, gate_s, ptmp_ref, *, rows_per_step):
    L = SSD_CHUNK
    R = rows_per_step

    @pl.when(pl.program_id(1) == 0)
    def _():
        state_ref[...] = jnp.zeros_like(state_ref)
        hist_ref[...] = jnp.zeros_like(hist_ref)
        cumcarry_ref[...] = jnp.zeros_like(cumcarry_ref)

    x = x_ref[...].reshape(R * L, D_MODEL)
    ms = jnp.mean(x * x, axis=-1, keepdims=True)
    h = (x * lax.rsqrt(ms + NORM_EPS) * mw_ref[...]).astype(BF16)

    def proj(c0, n):
        return jnp.dot(h, w_ref[:, c0:c0 + n], preferred_element_type=F32)

    z_s[...] = proj(_Z0, SSD_WIDTH).astype(BF16).reshape(R, L, SSD_WIDTH)
    xbc_s[...] = proj(_XBC0, SSD_XBC_WIDTH).astype(BF16).reshape(R, L, SSD_XBC_WIDTH)
    gate_s[...] = proj(_GATE0, LANES).reshape(R, L, LANES)

    def normed_heads(base, o_ref, hw_ref):
        ptmp_ref[...] = proj(base, FOX_WIDTH)
        g = g_ref[...]
        for c in range(0, FOX_WIDTH, MXU_DIM):
            t = ptmp_ref[:, c:c + MXU_DIM]
            ss = jnp.dot((t * t).astype(BF16), g, preferred_element_type=F32)
            o_ref[:, :, c:c + MXU_DIM] = (
                t * lax.rsqrt(ss * (1.0 / HEAD_DIM) + NORM_EPS) * hw_ref[...]
            ).astype(BF16).reshape(R, L, MXU_DIM)

    def values_transposed():
        ptmp_ref[...] = proj(_V0, FOX_WIDTH)
        for c in range(0, FOX_WIDTH, MXU_DIM):
            for r in range(R):
                vt_ref[r, c:c + MXU_DIM, :] = ptmp_ref[r * L:(r + 1) * L, c:c + MXU_DIM].T.astype(BF16)

    projections = [functools.partial(normed_heads, _Q0, q_ref, qw_ref),
                   functools.partial(normed_heads, _K0, k_ref, kw_ref),
                   values_transposed]
    chunks = [_ssd_chunk(r, xbc_s, z_s, gate_s, cw_ref, cb_ref, gbias_ref, alog_ref, dskip_ref,
                         nw_ref, sel_ref, shift_ref, y_ref, kbias_ref, cumt_ref,
                         state_ref, hist_ref, cumcarry_ref, xdd_ref, yg_ref) for r in range(R)]
    stage = 0
    while chunks:
        chunks = [c for c in chunks if next(c, _DONE) is not _DONE]
        if 1 <= stage <= len(projections):
            projections[stage - 1]()
        stage += 1


_DONE = object()


_CONV_HIST = 16


def _conv_shift_matrix():
    import numpy as np
    L = SSD_CHUNK
    m = np.zeros(((SSD_CONV - 1) * L, _CONV_HIST + L), np.float32)
    for j in range(1, SSD_CONV):
        for t in range(L):
            m[(j - 1) * L + t, _CONV_HIST + t - j] = 1.0
    return jnp.asarray(m, BF16)


_KBIAS_TERMS = 3


def _kbias_selector():
    import numpy as np
    sel = np.zeros((_KBIAS_TERMS * LANES, FOX_WIDTH), np.float32)
    for h in range(FOX_HEADS):
        base = (h // 2) * PAIR + (HEAD_DIM if h % 2 == 0 else 0)
        for i in range(_KBIAS_TERMS):
            sel[i * LANES + SSD_HEADS + h, base + i] = 1.0
    return jnp.asarray(sel, BF16)


def _mix_in(x, norm_w, w_r, g, qw, kw, conv_w, conv_b, gate_bias, alog_p, dskip_e, ssd_norm_w):
    b, s, _ = x.shape
    L = SSD_CHUNK
    rows = 2 if b % 2 == 0 else 1
    sel = _kbias_selector()
    shift = _conv_shift_matrix()
    blk = lambda n: pl.BlockSpec((rows, L, n), lambda i, j: (i, j, 0))
    consts = (norm_w, w_r, g, qw, kw, conv_w, conv_b, gate_bias, alog_p, dskip_e, ssd_norm_w, sel, shift)
    return pl.pallas_call(
        functools.partial(_mix_in_body, rows_per_step=rows),
        grid=(b // rows, s // L),
        in_specs=[blk(D_MODEL)] + [_const_spec(c.shape) for c in consts],
        out_specs=(blk(FOX_WIDTH), blk(FOX_WIDTH),
                   pl.BlockSpec((rows, FOX_WIDTH, L), lambda i, j: (i, 0, j)),
                   blk(SSD_WIDTH), blk(FOX_WIDTH),
                   pl.BlockSpec((rows, FOX_HEADS, L), lambda i, j: (i, 0, j))),
        out_shape=(jax.ShapeDtypeStruct((b, s, FOX_WIDTH), BF16),
                   jax.ShapeDtypeStruct((b, s, FOX_WIDTH), BF16),
                   jax.ShapeDtypeStruct((b, FOX_WIDTH, s), BF16),
                   jax.ShapeDtypeStruct((b, s, SSD_WIDTH), BF16),
                   jax.ShapeDtypeStruct((b, s, FOX_WIDTH), BF16),
                   jax.ShapeDtypeStruct((b, FOX_HEADS, s), F32)),
        scratch_shapes=[
            pltpu.VMEM((rows, SSD_STATE, SSD_WIDTH), F32),
            pltpu.VMEM((2, rows, _CONV_HIST, SSD_XBC_WIDTH), BF16),
            pltpu.VMEM((rows, SUBLANES, LANES), F32),
            pltpu.VMEM((rows, L, SSD_WIDTH), BF16),
            pltpu.VMEM((rows, L, SSD_WIDTH), F32),
            pltpu.VMEM((rows, L, SSD_XBC_WIDTH), BF16),
            pltpu.VMEM((rows, L, SSD_WIDTH), BF16),
            pltpu.VMEM((rows, L, LANES), F32),
            pltpu.VMEM((rows * L, FOX_WIDTH), F32),
        ],
        compiler_params=pltpu.CompilerParams(
            dimension_semantics=("arbitrary", "arbitrary"), vmem_limit_bytes=_VMEM_LIMIT),
        name="mix_in",
    )(x, *consts)


_VAUG_ROWS = HEAD_DIM + 16
_KV_UNROLL = 4


def _fox_body(q_ref, k_ref, vt_ref, kbias_ref, cumt_ref, o_ref,
              kaug_ref, vaug_ref, m_ref, acc_ref, st_a, st_b, *, blk):
    s = q_ref.shape[1]
    nblk = s // blk
    lane = lax.broadcasted_iota(jnp.int32, (blk, PAIR), 1).astype(F32).astype(BF16)
    lo_half = lane < HEAD_DIM

    def build_keys(i, _):
        r0 = pl.multiple_of(i * blk, blk)
        kp = k_ref[0, pl.ds(r0, blk), :]
        kb = kbias_ref[0, pl.ds(r0, blk), :]
        kaug_ref[0, pl.ds(r0, blk), :] = jnp.where(lo_half, kp, kb)
        kaug_ref[1, pl.ds(r0, blk), :] = jnp.where(lo_half, kb, kp)
        for h in range(2):
            vaug_ref[h, 0:HEAD_DIM, pl.ds(r0, blk)] = vt_ref[0, h * HEAD_DIM:(h + 1) * HEAD_DIM,
                                                             pl.ds(r0, blk)]
            vaug_ref[h, HEAD_DIM:_VAUG_ROWS, pl.ds(r0, blk)] = jnp.ones(
                (_VAUG_ROWS - HEAD_DIM, blk), BF16)
        return 0

    lax.fori_loop(0, nblk, build_keys, 0)

    one = jnp.ones((blk, PAIR), BF16)
    zero = jnp.zeros((blk, PAIR), BF16)
    q_ones = (jnp.where(lane < HEAD_DIM + _KBIAS_TERMS, one, zero),
              jnp.where(lane < _KBIAS_TERMS, one, zero))
    key_idx = lax.broadcasted_iota(jnp.int32, (blk, blk), 0)
    qry_idx = lax.broadcasted_iota(jnp.int32, (blk, blk), 1)
    causal = key_idx <= qry_idx

    def queries(qi):
        q = q_ref[0, pl.ds(pl.multiple_of(qi * blk, blk), blk), :]
        return (jnp.where(lo_half, q, q_ones[0]), jnp.where(lo_half, q_ones[1], q))

    def scores(h, j, qa, st_ref):
        k0 = pl.multiple_of(j * blk, blk)
        st_ref[...] = lax.dot_general(kaug_ref[h, pl.ds(k0, blk), :], qa[h],
                                      (((1,), (1,)), ((), ())), preferred_element_type=F32)

    scores(0, 0, queries(0), st_a)

    def q_block(qi, _):
        q0 = pl.multiple_of(qi * blk, blk)
        qa = queries(qi)
        cq = cumt_ref[0, 0, :, pl.ds(q0, blk)] * LOG2E
        m_ref[...] = jnp.full(m_ref.shape, NEG_BIG, F32)
        acc_ref[...] = jnp.zeros_like(acc_ref)

        def softmax_pv(h, k0, st, q_lo, masked):
            nk, nq = st.shape
            qs = slice(q_lo, q_lo + nq)
            if masked:
                st = jnp.where(causal[:nk, :nq], st, NEG_BIG)
            cqh = cq[h:h + 1, qs]
            m_prev = m_ref[h, :, qs]
            m_new = jnp.maximum(m_prev, jnp.max(st, axis=0, keepdims=True) + cqh)
            pt = jnp.exp2(st - (m_new - cqh)).astype(BF16)
            alpha = jnp.exp2(m_prev - m_new)
            acc_ref[h, :, qs] = alpha * acc_ref[h, :, qs] + jnp.dot(
                vaug_ref[h, :, pl.ds(pl.multiple_of(k0, nk), nk)], pt, preferred_element_type=F32)
            m_ref[h, :, qs] = m_new

        def off_diag(j):
            scores(1, j, qa, st_b)
            softmax_pv(0, j * blk, st_a[...], 0, False)
            scores(0, j + 1, qa, st_a)
            softmax_pv(1, j * blk, st_b[...], 0, False)

        def diagonal(h, st_ref):
            half = blk // 2
            softmax_pv(h, q0, st_ref[0:half, :], 0, True)
            softmax_pv(h, q0 + half, st_ref[half:, half:], half, True)

        def run_blocks(j0, n):
            for t in range(n):
                off_diag(j0 + t)

        def unrolled(jj, _):
            run_blocks(_KV_UNROLL * jj, _KV_UNROLL)
            return 0

        lax.fori_loop(0, qi // _KV_UNROLL, unrolled, 0)
        done = qi - lax.rem(qi, _KV_UNROLL)
        n = _KV_UNROLL // 2
        while n >= 1:
            pl.when(lax.bitwise_and(qi, n) != 0)(functools.partial(run_blocks, done, n))
            done = done + lax.bitwise_and(qi, n)
            n //= 2

        scores(1, qi, qa, st_b)
        diagonal(0, st_a)
        qi_next = jnp.minimum(qi + 1, nblk - 1)
        scores(0, 0, queries(qi_next), st_a)
        diagonal(1, st_b)

        out_t = jnp.concatenate(
            [acc_ref[h, 0:HEAD_DIM, :] * (1.0 / acc_ref[h, HEAD_DIM:HEAD_DIM + 1, :])
             for h in range(2)], axis=0)
        o_ref[0, pl.ds(q0, blk), :] = out_t.T.astype(BF16)
        return 0

    lax.fori_loop(0, nblk, q_block, 0)


def _fox_attention(q, k, vt, kbias, cumt_pairs, blk):
    b, s, _ = q.shape
    n_pairs = FOX_HEADS // 2
    pair_blk = pl.BlockSpec((1, s, PAIR), lambda i, j: (i, 0, j))
    return pl.pallas_call(
        functools.partial(_fox_body, blk=blk),
        grid=(b, n_pairs),
        in_specs=[pair_blk, pair_blk,
                  pl.BlockSpec((1, PAIR, s), lambda i, j: (i, j, 0)),
                  pair_blk,
                  pl.BlockSpec((1, 1, 2, s), lambda i, j: (i, j, 0, 0))],
        out_specs=pair_blk,
        out_shape=jax.ShapeDtypeStruct((b, s, FOX_WIDTH), BF16),
        scratch_shapes=[
            pltpu.VMEM((2, s, PAIR), BF16),
            pltpu.VMEM((2, _VAUG_ROWS, s), BF16),
            pltpu.VMEM((2, 1, blk), F32),
            pltpu.VMEM((2, _VAUG_ROWS, blk), F32),
            pltpu.VMEM((blk, blk), F32),
            pltpu.VMEM((blk, blk), F32),
        ],
        compiler_params=pltpu.CompilerParams(
            dimension_semantics=("arbitrary", "arbitrary"), vmem_limit_bytes=_VMEM_LIMIT),
        name="fox_attention",
    )(q, k, vt, kbias, cumt_pairs)


def _ffn_body(x_ref, ys_ref, yf_ref, wo_ref, nw_ref, wup_ref, cw_ref, cb_ref, wdn_ref,
              o_ref, carry_ref, hg_ref, hv_ref, act_ref, *, tm, ch):
    @pl.when(pl.program_id(1) == 0)
    def _():
        carry_ref[...] = jnp.zeros_like(carry_ref)

    x1 = (x_ref[0]
          + jnp.dot(ys_ref[0], wo_ref[0:SSD_WIDTH, :], preferred_element_type=F32)
          + jnp.dot(yf_ref[0], wo_ref[SSD_WIDTH:, :], preferred_element_type=F32))
    o_ref[0] = x1
    ms = jnp.mean(x1 * x1, axis=-1, keepdims=True)
    hf = (x1 * lax.rsqrt(ms + NORM_EPS) * nw_ref[...]).astype(BF16)

    hist = SUBLANES

    def conv_half(h_ref, c0):
        cs = slice(c0, c0 + ch)
        h_ref[0:hist, :] = carry_ref[:, cs]
        h_ref[hist:hist + tm, :] = jnp.dot(hf, wup_ref[:, cs], preferred_element_type=F32)
        carry_ref[:, cs] = h_ref[tm:tm + hist, :]
        out = cb_ref[:, cs] + cw_ref[FFN_CONV - 1:FFN_CONV, cs] * h_ref[hist:hist + tm, :]
        for k in range(FFN_CONV - 1):
            off = hist - (FFN_CONV - 1) + k
            out = out + cw_ref[k:k + 1, cs] * h_ref[off:off + tm, :]
        return out

    for c0 in range(0, D_FF, ch):
        gate = conv_half(hg_ref, c0)
        val = conv_half(hv_ref, D_FF + c0)
        act_ref[:, c0:c0 + ch] = (gate * (1.0 / (1.0 + jnp.exp(-gate))) * val).astype(BF16)

    o_ref[0] = o_ref[0] + jnp.dot(act_ref[...], wdn_ref[...], preferred_element_type=F32)


def _out_ffn(x, y_ssd, y_fox, w_out, norm_w, w_up, conv_w, conv_b, w_down, tm, ch):
    b, s, _ = x.shape
    blk = lambda n: pl.BlockSpec((1, tm, n), lambda i, j: (i, j, 0))
    return pl.pallas_call(
        functools.partial(_ffn_body, tm=tm, ch=ch),
        grid=(b, s // tm),
        in_specs=[blk(D_MODEL), blk(SSD_WIDTH), blk(FOX_WIDTH),
                  _const_spec(w_out.shape), _const_spec(norm_w.shape), _const_spec(w_up.shape),
                  _const_spec(conv_w.shape), _const_spec(conv_b.shape), _const_spec(w_down.shape)],
        out_specs=blk(D_MODEL),
        out_shape=jax.ShapeDtypeStruct((b, s, D_MODEL), F32),
        scratch_shapes=[
            pltpu.VMEM((SUBLANES, 2 * D_FF), F32),
            pltpu.VMEM((tm + SUBLANES, ch), F32),
            pltpu.VMEM((tm + SUBLANES, ch), F32),
            pltpu.VMEM((tm, D_FF), BF16),
        ],
        compiler_params=pltpu.CompilerParams(
            dimension_semantics=("arbitrary", "arbitrary"), vmem_limit_bytes=_VMEM_LIMIT),
        name="out_ffn",
    )(x, y_ssd, y_fox, w_out, norm_w, w_up, conv_w, conv_b, w_down)


def _pad_lanes(v):
    return jnp.pad(v.astype(F32), (0, LANES - v.shape[0]))[None, :]


def _layer(x, norm_mix_w, w_in, ssd_conv_w, ssd_conv_b, ssd_dt_bias, ssd_a_log, ssd_d, ssd_norm_w,
           fox_f_bias, fox_q_norm_w, fox_k_norm_w, w_out, norm_ffn_w, w_up, ffn_conv_w, ffn_conv_b,
           w_down, *, attn_blk, tm_ffn, ffn_ch):
    b, s, d = x.shape
    z_end = SSD_WIDTH
    xbc_end = z_end + SSD_XBC_WIDTH
    dt_end = xbc_end + SSD_HEADS
    q_end = dt_end + FOX_WIDTH
    k_end = q_end + FOX_WIDTH
    v_end = k_end + FOX_WIDTH

    def pad_cols(w):
        return jnp.pad(w, ((0, 0), (0, LANES - w.shape[1])))

    w_b = w_in.astype(BF16)
    w_r = jnp.concatenate(
        [w_b[:, :xbc_end], w_b[:, dt_end:v_end],
         pad_cols(jnp.concatenate([w_b[:, xbc_end:dt_end], w_b[:, v_end:]], axis=1))], axis=1)
    heads_per_tile = MXU_DIM // HEAD_DIM
    g = jnp.kron(jnp.eye(heads_per_tile, dtype=F32), jnp.ones((HEAD_DIM, HEAD_DIM), F32)).astype(BF16)
    qw = (jnp.tile(fox_q_norm_w.astype(F32), heads_per_tile) * (HEAD_DIM ** -0.5 * LOG2E))[None, :]
    kw = jnp.tile(fox_k_norm_w.astype(F32), heads_per_tile)[None, :]

    q, k, vt, y_ssd, kbias, cumt = _mix_in(
        x, norm_mix_w[None, :].astype(F32), w_r, g, qw, kw,
        ssd_conv_w.astype(F32), ssd_conv_b[None, :].astype(F32),
        _pad_lanes(jnp.concatenate([ssd_dt_bias, fox_f_bias])), _pad_lanes(ssd_a_log),
        jnp.repeat(ssd_d.astype(F32), HEAD_DIM)[None, :], ssd_norm_w[None, :].astype(F32))

    cumt_pairs = cumt.reshape(b, FOX_HEADS // 2, 2, s)
    y_fox = _fox_attention(q, k, vt, kbias, cumt_pairs, attn_blk)

    return _out_ffn(x, y_ssd, y_fox, w_out.astype(BF16), norm_ffn_w[None, :].astype(F32),
                    w_up.astype(BF16), ffn_conv_w.astype(F32), ffn_conv_b[None, :].astype(F32),
                    w_down.astype(BF16), tm_ffn, ffn_ch)


def kernel(x, norm_mix_w, w_in, ssd_conv_w, ssd_conv_b, ssd_dt_bias, ssd_a_log, ssd_d, ssd_norm_w,
           fox_f_bias, fox_q_norm_w, fox_k_norm_w, w_out, norm_ffn_w, w_up, ffn_conv_w, ffn_conv_b,
           w_down):
    depth = w_in.shape[0]
    for layer in range(depth):
        x = _layer(x, norm_mix_w[layer], w_in[layer], ssd_conv_w[layer], ssd_conv_b[layer],
                   ssd_dt_bias[layer], ssd_a_log[layer], ssd_d[layer], ssd_norm_w[layer],
                   fox_f_bias[layer], fox_q_norm_w[layer], fox_k_norm_w[layer], w_out[layer],
                   norm_ffn_w[layer], w_up[layer], ffn_conv_w[layer], ffn_conv_b[layer],
                   w_down[layer], attn_blk=512, tm_ffn=512, ffn_ch=256)
    return x
```

```python
import functools
import math

import jax
import jax.numpy as jnp
from jax import lax
from jax.experimental import pallas as pl
from jax.experimental.pallas import tpu as pltpu

F32 = jnp.float32
BF16 = jnp.bfloat16

D_MODEL = 1024
HEAD_DIM = 64
SSD_HEADS = 16
SSD_GROUPS = 2
SSD_STATE = 128
SSD_CONV = 4
SSD_CHUNK = 128
SSD_WIDTH = SSD_HEADS * HEAD_DIM
SSD_BC_WIDTH = SSD_GROUPS * SSD_STATE
SSD_XBC_WIDTH = SSD_WIDTH + 2 * SSD_BC_WIDTH
FOX_HEADS = 16
FOX_WIDTH = FOX_HEADS * HEAD_DIM
D_FF = 2816
FFN_CONV = 3
NORM_EPS = 1e-6
LOG2E = math.log2(math.e)

LANES = 128
SUBLANES = 8
MXU_DIM = 256
PAIR = 2 * HEAD_DIM
NEG_BIG = -1e30

_Z0 = 0
_XBC0 = _Z0 + SSD_WIDTH
_Q0 = _XBC0 + SSD_XBC_WIDTH
_K0 = _Q0 + FOX_WIDTH
_V0 = _K0 + FOX_WIDTH
_GATE0 = _V0 + FOX_WIDTH
_IN_COLS_PADDED = _GATE0 + LANES

_VMEM_LIMIT = 56 * 1024 * 1024


def _const_spec(shape):
    zeros = (0,) * len(shape)
    return pl.BlockSpec(shape, lambda *_: zeros, pipeline_mode=pl.Buffered(1))


def _softplus(x):
    e = jnp.exp(-jnp.abs(x))
    u = 1.0 + e
    return jnp.maximum(x, 0.0) + (jnp.log(u) - ((u - 1.0) - e) / u)


def _split3(v):
    hi = v.astype(BF16)
    r1 = v - hi.astype(F32)
    mid = r1.astype(BF16)
    lo = (r1 - mid.astype(F32)).astype(BF16)
    return hi, mid, lo


def _ssd_chunk(r, xbc_ref, z_ref, gate_ref, cw_ref, cb_ref, gbias_ref, alog_ref, dskip_ref, nw_ref,
               sel_ref, shift_ref, y_ref, kbias_ref, cumt_ref,
               state_ref, hist_ref, cumcarry_ref, xdd_ref, yg_ref):
    L = SSD_CHUNK

    x_raw = xbc_ref[r]
    slot = lax.rem(pl.program_id(1), 2)
    hist = hist_ref[slot, r]
    hist_ref[1 - slot, r] = x_raw[L - _CONV_HIST:, :]
    x_ext = jnp.concatenate([hist, x_raw], axis=0)
    shifted = jnp.dot(shift_ref[...], x_ext, preferred_element_type=F32)
    conv = cb_ref[...] + cw_ref[SSD_CONV - 1:SSD_CONV, :] * x_raw.astype(F32)
    for j in range(1, SSD_CONV):
        k = SSD_CONV - 1 - j
        conv = conv + cw_ref[k:k + 1, :] * shifted[(j - 1) * L:j * L, :]
    xbc = conv * (1.0 / (1.0 + jnp.exp(-conv)))
    yield

    xs = xbc[:, :SSD_WIDTH]
    bm = xbc[:, SSD_WIDTH:SSD_WIDTH + SSD_BC_WIDTH]
    cm = xbc[:, SSD_WIDTH + SSD_BC_WIDTH:]

    lane_row = lax.broadcasted_iota(jnp.int32, (1, LANES), 1)
    is_dt = lane_row < SSD_HEADS
    is_gate = lane_row < SSD_HEADS + FOX_HEADS

    def f_lanes(v):
        return jnp.where(is_dt, 0.0, jnp.where(is_gate, v, 0.0))

    sp = _softplus(jnp.where(is_dt, 1.0, -1.0) * (gate_ref[r] + gbias_ref[...]))
    coef = jnp.where(is_dt, -jnp.exp(alog_ref[...]), f_lanes(-1.0))
    steps = sp * coef
    dt = sp

    rows = lax.broadcasted_iota(jnp.int32, (L, L), 0)
    cols = lax.broadcasted_iota(jnp.int32, (L, L), 1)
    tri = rows >= cols
    tri_b = jnp.where(tri, 1.0, 0.0).astype(BF16)
    sums = jnp.dot(tri_b, jnp.concatenate(_split3(steps), axis=1), preferred_element_type=F32)
    a_cs = sums[:, 0:LANES] + sums[:, LANES:2 * LANES] + sums[:, 2 * LANES:3 * LANES]
    cum = a_cs + cumcarry_ref[r, 0:1, :]
    cumcarry_ref[r, 0:1, :] = f_lanes(cum[L - 1:L, :])
    cumt_ref[r] = cum.T
    kb = jnp.concatenate(_split3(cum * (-LOG2E)), axis=1)
    kbias_ref[r] = jnp.dot(kb, sel_ref[...], preferred_element_type=F32).astype(BF16)

    a2 = a_cs * LOG2E
    a2t = a2.T

    def col(v, h):
        return jnp.broadcast_to(v[:, h:h + 1], (L, LANES))

    lane = lax.broadcasted_iota(jnp.int32, (L, PAIR), 1)
    lo_half = lane < HEAD_DIM
    heads_per_group = SSD_HEADS // SSD_GROUPS
    pairs_per_group = heads_per_group // 2

    cbs, cms, bms = [], [], []
    for g in range(SSD_GROUPS):
        cm_g = cm[:, g * SSD_STATE:(g + 1) * SSD_STATE]
        bm_g = bm[:, g * SSD_STATE:(g + 1) * SSD_STATE]
        cbs.append(lax.dot_general(cm_g.astype(BF16), bm_g.astype(BF16),
                                   (((1,), (1,)), ((), ())), preferred_element_type=F32))
        cms.append(cm_g)
        bms.append(bm_g)
    yield

    ssq = [None] * SSD_GROUPS
    cd_parts = []
    for p in range(SSD_HEADS // 2):
        g = p // pairs_per_group
        h0, h1 = 2 * p, 2 * p + 1
        sl = slice(p * PAIR, (p + 1) * PAIR)
        xs_p = xs[:, sl]
        acs = (col(a2, h0), col(a2, h1))
        a_pair = jnp.where(lo_half, acs[0], acs[1])
        a_end = a_pair[L - 1:L, :]
        xdt_p = xs_p * jnp.where(lo_half, col(dt, h0), col(dt, h1))
        xdd_ref[r, :, sl] = (xdt_p * jnp.exp2(a_end - a_pair)).astype(BF16)
        cd_parts.append(jnp.exp2(a_end))
        rhs = jnp.concatenate([xdt_p.astype(BF16), state_ref[r, :, sl].astype(BF16)], axis=0)
        ys = []
        for h, ac in zip((h0, h1), acs):
            decay = jnp.exp2(jnp.where(tri, ac - a2t[h:h + 1, :], NEG_BIG))
            lhs = jnp.concatenate([(cbs[g] * decay).astype(BF16),
                                   (cms[g] * jnp.exp2(ac)).astype(BF16)], axis=1)
            ys.append(jnp.dot(lhs, rhs, preferred_element_type=F32))
        y_p = jnp.where(lo_half, ys[0], ys[1]) + dskip_ref[:, sl] * xs_p
        zp = z_ref[r, :, sl].astype(F32)
        yg = y_p * (zp * (1.0 / (1.0 + jnp.exp(-zp))))
        yg_ref[r, :, sl] = yg
        s2 = jnp.sum(yg * yg, axis=-1, keepdims=True)
        ssq[g] = s2 if ssq[g] is None else ssq[g] + s2
        if p % 2 == 1:
            yield

    gw = SSD_WIDTH // SSD_GROUPS
    cd_row = jnp.concatenate(cd_parts, axis=1)
    for g in range(SSD_GROUPS):
        gs = slice(g * gw, (g + 1) * gw)
        scale = lax.rsqrt(ssq[g] * (1.0 / gw) + NORM_EPS)
        y_ref[r, :, gs] = (yg_ref[r, :, gs] * scale * nw_ref[:, gs]).astype(BF16)
        upd = jnp.dot(bms[g].T.astype(BF16), xdd_ref[r, :, gs], preferred_element_type=F32)
        state_ref[r, :, gs] = state_ref[r, :, gs] * cd_row[:, gs] + upd


def _mix_in_body(x_ref, mw_ref, w_ref, g_ref, qw_ref, kw_ref,
                 cw_ref, cb_ref, gbias_ref, alog_ref, dskip_ref, nw_ref, sel_ref, shift_ref,
                 q_ref, k_ref, vt_ref, y_ref, kbias_ref, cumt_ref,
                 state_ref, hist_ref, cumcarry_ref, xdd_ref, yg_ref,
                 xbc_s, z_s, gate_s, ptmp_ref, *, rows_per_step):
    L = SSD_CHUNK
    R = rows_per_step

    @pl.when(pl.program_id(1) == 0)
    def _():
        state_ref[...] = jnp.zeros_like(state_ref)
        hist_ref[...] = jnp.zeros_like(hist_ref)
        cumcarry_ref[...] = jnp.zeros_like(cumcarry_ref)

    x = x_ref[...].reshape(R * L, D_MODEL)
    ms = jnp.mean(x * x, axis=-1, keepdims=True)
    h = (x * lax.rsqrt(ms + NORM_EPS) * mw_ref[...]).astype(BF16)

    def proj(c0, n):
        return jnp.dot(h, w_ref[:, c0:c0 + n], preferred_element_type=F32)

    z_s[...] = proj(_Z0, SSD_WIDTH).astype(BF16).reshape(R, L, SSD_WIDTH)
    xbc_s[...] = proj(_XBC0, SSD_XBC_WIDTH).astype(BF16).reshape(R, L, SSD_XBC_WIDTH)
    gate_s[...] = proj(_GATE0, LANES).reshape(R, L, LANES)

    def normed_heads(base, o_ref, hw_ref):
        ptmp_ref[...] = proj(base, FOX_WIDTH)
        g = g_ref[...]
        for c in range(0, FOX_WIDTH, MXU_DIM):
            t = ptmp_ref[:, c:c + MXU_DIM]
            ss = jnp.dot((t * t).astype(BF16), g, preferred_element_type=F32)
            o_ref[:, :, c:c + MXU_DIM] = (
                t * lax.rsqrt(ss * (1.0 / HEAD_DIM) + NORM_EPS) * hw_ref[...]
            ).astype(BF16).reshape(R, L, MXU_DIM)

    def values_transposed():
        ptmp_ref[...] = proj(_V0, FOX_WIDTH)
        for c in range(0, FOX_WIDTH, MXU_DIM):
            for r in range(R):
                vt_ref[r, c:c + MXU_DIM, :] = ptmp_ref[r * L:(r + 1) * L, c:c + MXU_DIM].T.astype(BF16)

    projections = [functools.partial(normed_heads, _Q0, q_ref, qw_ref),
                   functools.partial(normed_heads, _K0, k_ref, kw_ref),
                   values_transposed]
    chunks = [_ssd_chunk(r, xbc_s, z_s, gate_s, cw_ref, cb_ref, gbias_ref, alog_ref, dskip_ref,
                         nw_ref, sel_ref, shift_ref, y_ref, kbias_ref, cumt_ref,
                         state_ref, hist_ref, cumcarry_ref, xdd_ref, yg_ref) for r in range(R)]
    stage = 0
    while chunks:
        chunks = [c for c in chunks if next(c, _DONE) is not _DONE]
        if 1 <= stage <= len(projections):
            projections[stage - 1]()
        stage += 1


_DONE = object()


_CONV_HIST = 16


def _conv_shift_matrix():
    import numpy as np
    L = SSD_CHUNK
    m = np.zeros(((SSD_CONV - 1) * L, _CONV_HIST + L), np.float32)
    for j in range(1, SSD_CONV):
        for t in range(L):
            m[(j - 1) * L + t, _CONV_HIST + t - j] = 1.0
    return jnp.asarray(m, BF16)


_KBIAS_TERMS = 3


def _kbias_selector():
    import numpy as np
    sel = np.zeros((_KBIAS_TERMS * LANES, FOX_WIDTH), np.float32)
    for h in range(FOX_HEADS):
        base = (h // 2) * PAIR + (HEAD_DIM if h % 2 == 0 else 0)
        for i in range(_KBIAS_TERMS):
            sel[i * LANES + SSD_HEADS + h, base + i] = 1.0
    return jnp.asarray(sel, BF16)


def _mix_in(x, norm_w, w_r, g, qw, kw, conv_w, conv_b, gate_bias, alog_p, dskip_e, ssd_norm_w):
    b, s, _ = x.shape
    L = SSD_CHUNK
    rows = 2 if b % 2 == 0 else 1
    sel = _kbias_selector()
    shift = _conv_shift_matrix()
    blk = lambda n: pl.BlockSpec((rows, L, n), lambda i, j: (i, j, 0))
    consts = (norm_w, w_r, g, qw, kw, conv_w, conv_b, gate_bias, alog_p, dskip_e, ssd_norm_w, sel, shift)
    return pl.pallas_call(
        functools.partial(_mix_in_body, rows_per_step=rows),
        grid=(b // rows, s // L),
        in_specs=[blk(D_MODEL)] + [_const_spec(c.shape) for c in consts],
        out_specs=(blk(FOX_WIDTH), blk(FOX_WIDTH),
                   pl.BlockSpec((rows, FOX_WIDTH, L), lambda i, j: (i, 0, j)),
                   blk(SSD_WIDTH), blk(FOX_WIDTH),
                   pl.BlockSpec((rows, LANES, L), lambda i, j: (i, 0, j))),
        out_shape=(jax.ShapeDtypeStruct((b, s, FOX_WIDTH), BF16),
                   jax.ShapeDtypeStruct((b, s, FOX_WIDTH), BF16),
                   jax.ShapeDtypeStruct((b, FOX_WIDTH, s), BF16),
                   jax.ShapeDtypeStruct((b, s, SSD_WIDTH), BF16),
                   jax.ShapeDtypeStruct((b, s, FOX_WIDTH), BF16),
                   jax.ShapeDtypeStruct((b, LANES, s), F32)),
        scratch_shapes=[
            pltpu.VMEM((rows, SSD_STATE, SSD_WIDTH), F32),
            pltpu.VMEM((2, rows, _CONV_HIST, SSD_XBC_WIDTH), BF16),
            pltpu.VMEM((rows, SUBLANES, LANES), F32),
            pltpu.VMEM((rows, L, SSD_WIDTH), BF16),
            pltpu.VMEM((rows, L, SSD_WIDTH), F32),
            pltpu.VMEM((rows, L, SSD_XBC_WIDTH), BF16),
            pltpu.VMEM((rows, L, SSD_WIDTH), BF16),
            pltpu.VMEM((rows, L, LANES), F32),
            pltpu.VMEM((rows * L, FOX_WIDTH), F32),
        ],
        compiler_params=pltpu.CompilerParams(
            dimension_semantics=("arbitrary", "arbitrary"), vmem_limit_bytes=_VMEM_LIMIT),
        name="mix_in",
    )(x, *consts)


_VAUG_ROWS = HEAD_DIM + 16
_KV_UNROLL = 4


def _fox_body(q_ref, k_ref, vt_ref, kbias_ref, cumt_ref, o_ref,
              kaug_ref, vaug_ref, m_ref, acc_ref, st_a, st_b, *, blk):
    s = q_ref.shape[1]
    nblk = s // blk
    lane = lax.broadcasted_iota(jnp.int32, (blk, PAIR), 1).astype(F32).astype(BF16)
    lo_half = lane < HEAD_DIM

    def build_keys(i, _):
        r0 = pl.multiple_of(i * blk, blk)
        kp = k_ref[0, pl.ds(r0, blk), :]
        kb = kbias_ref[0, pl.ds(r0, blk), :]
        kaug_ref[0, pl.ds(r0, blk), :] = jnp.where(lo_half, kp, kb)
        kaug_ref[1, pl.ds(r0, blk), :] = jnp.where(lo_half, kb, kp)
        for h in range(2):
            vaug_ref[h, 0:HEAD_DIM, pl.ds(r0, blk)] = vt_ref[0, h * HEAD_DIM:(h + 1) * HEAD_DIM,
                                                             pl.ds(r0, blk)]
            vaug_ref[h, HEAD_DIM:_VAUG_ROWS, pl.ds(r0, blk)] = jnp.ones(
                (_VAUG_ROWS - HEAD_DIM, blk), BF16)
        return 0

    lax.fori_loop(0, nblk, build_keys, 0)

    one = jnp.ones((blk, PAIR), BF16)
    zero = jnp.zeros((blk, PAIR), BF16)
    q_ones = (jnp.where(lane < HEAD_DIM + _KBIAS_TERMS, one, zero),
              jnp.where(lane < _KBIAS_TERMS, one, zero))
    key_idx = lax.broadcasted_iota(jnp.int32, (blk, blk), 0)
    qry_idx = lax.broadcasted_iota(jnp.int32, (blk, blk), 1)
    causal = key_idx <= qry_idx

    def queries(qi):
        q = q_ref[0, pl.ds(pl.multiple_of(qi * blk, blk), blk), :]
        return (jnp.where(lo_half, q, q_ones[0]), jnp.where(lo_half, q_ones[1], q))

    def scores(h, j, qa, st_ref):
        k0 = pl.multiple_of(j * blk, blk)
        st_ref[...] = lax.dot_general(kaug_ref[h, pl.ds(k0, blk), :], qa[h],
                                      (((1,), (1,)), ((), ())), preferred_element_type=F32)

    scores(0, 0, queries(0), st_a)

    def q_block(qi, _):
        q0 = pl.multiple_of(qi * blk, blk)
        qa = queries(qi)
        cq = cumt_ref[0, 0, :, pl.ds(q0, blk)] * LOG2E
        m_ref[...] = jnp.full(m_ref.shape, NEG_BIG, F32)
        acc_ref[...] = jnp.zeros_like(acc_ref)

        def softmax_pv(h, k0, st, q_lo, masked):
            nk, nq = st.shape
            qs = slice(q_lo, q_lo + nq)
            if masked:
                st = jnp.where(causal[:nk, :nq], st, NEG_BIG)
            cqh = cq[h:h + 1, qs]
            m_prev = m_ref[h, :, qs]
            m_new = jnp.maximum(m_prev, jnp.max(st, axis=0, keepdims=True) + cqh)
            pt = jnp.exp2(st - (m_new - cqh)).astype(BF16)
            alpha = jnp.exp2(m_prev - m_new)
            acc_ref[h, :, qs] = alpha * acc_ref[h, :, qs] + jnp.dot(
                vaug_ref[h, :, pl.ds(pl.multiple_of(k0, nk), nk)], pt, preferred_element_type=F32)
            m_ref[h, :, qs] = m_new

        def off_diag(j):
            scores(1, j, qa, st_b)
            softmax_pv(0, j * blk, st_a[...], 0, False)
            scores(0, j + 1, qa, st_a)
            softmax_pv(1, j * blk, st_b[...], 0, False)

        def diagonal(h, st_ref):
            half = blk // 2
            softmax_pv(h, q0, st_ref[0:half, :], 0, True)
            softmax_pv(h, q0 + half, st_ref[half:, half:], half, True)

        def run_blocks(j0, n):
            for t in range(n):
                off_diag(j0 + t)

        def unrolled(jj, _):
            run_blocks(_KV_UNROLL * jj, _KV_UNROLL)
            return 0

        lax.fori_loop(0, qi // _KV_UNROLL, unrolled, 0)
        done = qi - lax.rem(qi, _KV_UNROLL)
        n = _KV_UNROLL // 2
        while n >= 1:
            pl.when(lax.bitwise_and(qi, n) != 0)(functools.partial(run_blocks, done, n))
            done = done + lax.bitwise_and(qi, n)
            n //= 2

        scores(1, qi, qa, st_b)
        diagonal(0, st_a)
        qi_next = jnp.minimum(qi + 1, nblk - 1)
        scores(0, 0, queries(qi_next), st_a)
        diagonal(1, st_b)

        out_t = jnp.concatenate(
            [acc_ref[h, 0:HEAD_DIM, :] * (1.0 / acc_ref[h, HEAD_DIM:HEAD_DIM + 1, :])
             for h in range(2)], axis=0)
        o_ref[0, pl.ds(q0, blk), :] = out_t.T.astype(BF16)
        return 0

    lax.fori_loop(0, nblk, q_block, 0)


def _fox_attention(q, k, vt, kbias, cumt_pairs, blk):
    b, s, _ = q.shape
    n_pairs = FOX_HEADS // 2
    pair_blk = pl.BlockSpec((1, s, PAIR), lambda i, j: (i, 0, j))
    return pl.pallas_call(
        functools.partial(_fox_body, blk=blk),
        grid=(b, n_pairs),
        in_specs=[pair_blk, pair_blk,
                  pl.BlockSpec((1, PAIR, s), lambda i, j: (i, j, 0)),
                  pair_blk,
                  pl.BlockSpec((1, 1, 2, s), lambda i, j: (i, j, 0, 0))],
        out_specs=pair_blk,
        out_shape=jax.ShapeDtypeStruct((b, s, FOX_WIDTH), BF16),
        scratch_shapes=[
            pltpu.VMEM((2, s, PAIR), BF16),
            pltpu.VMEM((2, _VAUG_ROWS, s), BF16),
            pltpu.VMEM((2, 1, blk), F32),
            pltpu.VMEM((2, _VAUG_ROWS, blk), F32),
            pltpu.VMEM((blk, blk), F32),
            pltpu.VMEM((blk, blk), F32),
        ],
        compiler_params=pltpu.CompilerParams(
            dimension_semantics=("arbitrary", "arbitrary"), vmem_limit_bytes=_VMEM_LIMIT),
        name="fox_attention",
    )(q, k, vt, kbias, cumt_pairs)


def _ffn_body(x_ref, ys_ref, yf_ref, wo_ref, nw_ref, wup_ref, cw_ref, cb_ref, wdn_ref,
              o_ref, carry_ref, hg_ref, hv_ref, act_ref, *, tm, ch):
    @pl.when(pl.program_id(1) == 0)
    def _():
        carry_ref[...] = jnp.zeros_like(carry_ref)

    x1 = (x_ref[0]
          + jnp.dot(ys_ref[0], wo_ref[0:SSD_WIDTH, :], preferred_element_type=F32)
          + jnp.dot(yf_ref[0], wo_ref[SSD_WIDTH:, :], preferred_element_type=F32))
    o_ref[0] = x1
    ms = jnp.mean(x1 * x1, axis=-1, keepdims=True)
    hf = (x1 * lax.rsqrt(ms + NORM_EPS) * nw_ref[...]).astype(BF16)

    hist = SUBLANES

    def conv_half(h_ref, c0):
        cs = slice(c0, c0 + ch)
        h_ref[0:hist, :] = carry_ref[:, cs]
        h_ref[hist:hist + tm, :] = jnp.dot(hf, wup_ref[:, cs], preferred_element_type=F32)
        carry_ref[:, cs] = h_ref[tm:tm + hist, :]
        out = cb_ref[:, cs] + cw_ref[FFN_CONV - 1:FFN_CONV, cs] * h_ref[hist:hist + tm, :]
        for k in range(FFN_CONV - 1):
            off = hist - (FFN_CONV - 1) + k
            out = out + cw_ref[k:k + 1, cs] * h_ref[off:off + tm, :]
        return out

    for c0 in range(0, D_FF, ch):
        gate = conv_half(hg_ref, c0)
        val = conv_half(hv_ref, D_FF + c0)
        act_ref[:, c0:c0 + ch] = (gate * (1.0 / (1.0 + jnp.exp(-gate))) * val).astype(BF16)

    o_ref[0] = o_ref[0] + jnp.dot(act_ref[...], wdn_ref[...], preferred_element_type=F32)


def _out_ffn(x, y_ssd, y_fox, w_out, norm_w, w_up, conv_w, conv_b, w_down, tm, ch):
    b, s, _ = x.shape
    blk = lambda n: pl.BlockSpec((1, tm, n), lambda i, j: (i, j, 0))
    return pl.pallas_call(
        functools.partial(_ffn_body, tm=tm, ch=ch),
        grid=(b, s // tm),
        in_specs=[blk(D_MODEL), blk(SSD_WIDTH), blk(FOX_WIDTH),
                  _const_spec(w_out.shape), _const_spec(norm_w.shape), _const_spec(w_up.shape),
                  _const_spec(conv_w.shape), _const_spec(conv_b.shape), _const_spec(w_down.shape)],
        out_specs=blk(D_MODEL),
        out_shape=jax.ShapeDtypeStruct((b, s, D_MODEL), F32),
        scratch_shapes=[
            pltpu.VMEM((SUBLANES, 2 * D_FF), F32),
            pltpu.VMEM((tm + SUBLANES, ch), F32),
            pltpu.VMEM((tm + SUBLANES, ch), F32),
            pltpu.VMEM((tm, D_FF), BF16),
        ],
        compiler_params=pltpu.CompilerParams(
            dimension_semantics=("arbitrary", "arbitrary"), vmem_limit_bytes=_VMEM_LIMIT),
        name="out_ffn",
    )(x, y_ssd, y_fox, w_out, norm_w, w_up, conv_w, conv_b, w_down)


def _pad_lanes(v):
    return jnp.pad(v.astype(F32), (0, LANES - v.shape[0]))[None, :]


def _layer(x, norm_mix_w, w_in, ssd_conv_w, ssd_conv_b, ssd_dt_bias, ssd_a_log, ssd_d, ssd_norm_w,
           fox_f_bias, fox_q_norm_w, fox_k_norm_w, w_out, norm_ffn_w, w_up, ffn_conv_w, ffn_conv_b,
           w_down, *, attn_blk, tm_ffn, ffn_ch):
    b, s, d = x.shape
    z_end = SSD_WIDTH
    xbc_end = z_end + SSD_XBC_WIDTH
    dt_end = xbc_end + SSD_HEADS
    q_end = dt_end + FOX_WIDTH
    k_end = q_end + FOX_WIDTH
    v_end = k_end + FOX_WIDTH

    def pad_cols(w):
        return jnp.pad(w, ((0, 0), (0, LANES - w.shape[1])))

    w_r = jnp.concatenate(
        [w_in[:, :xbc_end], w_in[:, dt_end:v_end],
         pad_cols(jnp.concatenate([w_in[:, xbc_end:dt_end], w_in[:, v_end:]], axis=1))],
        axis=1).astype(BF16)
    heads_per_tile = MXU_DIM // HEAD_DIM
    g = jnp.kron(jnp.eye(heads_per_tile, dtype=F32), jnp.ones((HEAD_DIM, HEAD_DIM), F32)).astype(BF16)
    qw = (jnp.tile(fox_q_norm_w.astype(F32), heads_per_tile) * (HEAD_DIM ** -0.5 * LOG2E))[None, :]
    kw = jnp.tile(fox_k_norm_w.astype(F32), heads_per_tile)[None, :]

    q, k, vt, y_ssd, kbias, cumt = _mix_in(
        x, norm_mix_w[None, :].astype(F32), w_r, g, qw, kw,
        ssd_conv_w.astype(F32), ssd_conv_b[None, :].astype(F32),
        _pad_lanes(jnp.concatenate([ssd_dt_bias, fox_f_bias])), _pad_lanes(ssd_a_log),
        jnp.repeat(ssd_d.astype(F32), HEAD_DIM)[None, :], ssd_norm_w[None, :].astype(F32))

    cumt_pairs = cumt[:, SSD_HEADS:SSD_HEADS + FOX_HEADS, :].reshape(b, FOX_HEADS // 2, 2, s)
    y_fox = _fox_attention(q, k, vt, kbias, cumt_pairs, attn_blk)

    return _out_ffn(x, y_ssd, y_fox, w_out.astype(BF16), norm_ffn_w[None, :].astype(F32),
                    w_up.astype(BF16), ffn_conv_w.astype(F32), ffn_conv_b[None, :].astype(F32),
                    w_down.astype(BF16), tm_ffn, ffn_ch)


def kernel(x, norm_mix_w, w_in, ssd_conv_w, ssd_conv_b, ssd_dt_bias, ssd_a_log, ssd_d, ssd_norm_w,
           fox_f_bias, fox_q_norm_w, fox_k_norm_w, w_out, norm_ffn_w, w_up, ffn_conv_w, ffn_conv_b,
           w_down):
    depth = w_in.shape[0]
    for layer in range(depth):
        x = _layer(x, norm_mix_w[layer], w_in[layer], ssd_conv_w[layer], ssd_conv_b[layer],
                   ssd_dt_bias[layer], ssd_a_log[layer], ssd_d[layer], ssd_norm_w[layer],
                   fox_f_bias[layer], fox_q_norm_w[layer], fox_k_norm_w[layer], w_out[layer],
                   norm_ffn_w[layer], w_up[layer], ffn_conv_w[layer], ffn_conv_b[layer],
                   w_down[layer], attn_blk=512, tm_ffn=512, ffn_ch=256)
    return x
```

```python
import functools
import math

import jax
import jax.numpy as jnp
from jax import lax
from jax.experimental import pallas as pl
from jax.experimental.pallas import tpu as pltpu

F32 = jnp.float32
BF16 = jnp.bfloat16

D_MODEL = 1024
HEAD_DIM = 64
SSD_HEADS = 16
SSD_GROUPS = 2
SSD_STATE = 128
SSD_CONV = 4
SSD_CHUNK = 128
SSD_WIDTH = SSD_HEADS * HEAD_DIM
SSD_BC_WIDTH = SSD_GROUPS * SSD_STATE
SSD_XBC_WIDTH = SSD_WIDTH + 2 * SSD_BC_WIDTH
FOX_HEADS = 16
FOX_WIDTH = FOX_HEADS * HEAD_DIM
D_FF = 2816
FFN_CONV = 3
NORM_EPS = 1e-6
LOG2E = math.log2(math.e)

LANES = 128
SUBLANES = 8
MXU_DIM = 256
PAIR = 2 * HEAD_DIM
NEG_BIG = -1e30

_Z0 = 0
_XBC0 = _Z0 + SSD_WIDTH
_Q0 = _XBC0 + SSD_XBC_WIDTH
_K0 = _Q0 + FOX_WIDTH
_V0 = _K0 + FOX_WIDTH
_GATE0 = _V0 + FOX_WIDTH
_IN_COLS_PADDED = _GATE0 + LANES

_VMEM_LIMIT = 56 * 1024 * 1024


def _const_spec(shape):
    zeros = (0,) * len(shape)
    return pl.BlockSpec(shape, lambda *_: zeros, pipeline_mode=pl.Buffered(1))


def _softplus(x):
    e = jnp.exp(-jnp.abs(x))
    u = 1.0 + e
    return jnp.maximum(x, 0.0) + (jnp.log(u) - ((u - 1.0) - e) / u)


def _split3(v):
    hi = v.astype(BF16)
    r1 = v - hi.astype(F32)
    mid = r1.astype(BF16)
    lo = (r1 - mid.astype(F32)).astype(BF16)
    return hi, mid, lo


def _ssd_chunk(r, xbc_ref, z_ref, gate_ref, cw_ref, cb_ref, gbias_ref, alog_ref, dskip_ref, nw_ref,
               sel_ref, shift_ref, y_ref, kbias_ref, cumt_ref,
               state_ref, hist_ref, cumcarry_ref, xdd_ref, yg_ref):
    L = SSD_CHUNK

    x_raw = xbc_ref[r]
    slot = lax.rem(pl.program_id(1), 2)
    hist = hist_ref[slot, r]
    hist_ref[1 - slot, r] = x_raw[L - _CONV_HIST:, :]
    x_ext = jnp.concatenate([hist, x_raw], axis=0)
    shifted = jnp.dot(shift_ref[...], x_ext, preferred_element_type=F32)
    conv = cb_ref[...] + cw_ref[SSD_CONV - 1:SSD_CONV, :] * x_raw.astype(F32)
    for j in range(1, SSD_CONV):
        k = SSD_CONV - 1 - j
        conv = conv + cw_ref[k:k + 1, :] * shifted[(j - 1) * L:j * L, :]
    xbc = conv * (1.0 / (1.0 + jnp.exp(-conv)))
    yield

    xs = xbc[:, :SSD_WIDTH]
    bm = xbc[:, SSD_WIDTH:SSD_WIDTH + SSD_BC_WIDTH]
    cm = xbc[:, SSD_WIDTH + SSD_BC_WIDTH:]

    lane_row = lax.broadcasted_iota(jnp.int32, (1, LANES), 1)
    is_dt = lane_row < SSD_HEADS
    is_gate = lane_row < SSD_HEADS + FOX_HEADS

    def f_lanes(v):
        return jnp.where(is_dt, 0.0, jnp.where(is_gate, v, 0.0))

    sp = _softplus(jnp.where(is_dt, 1.0, -1.0) * (gate_ref[r] + gbias_ref[...]))
    coef = jnp.where(is_dt, -jnp.exp(alog_ref[...]), f_lanes(-1.0))
    steps = sp * coef
    dt = sp

    rows = lax.broadcasted_iota(jnp.int32, (L, L), 0)
    cols = lax.broadcasted_iota(jnp.int32, (L, L), 1)
    tri = rows >= cols
    tri_b = jnp.where(tri, 1.0, 0.0).astype(BF16)
    sums = jnp.dot(tri_b, jnp.concatenate(_split3(steps), axis=1), preferred_element_type=F32)
    a_cs = sums[:, 0:LANES] + sums[:, LANES:2 * LANES] + sums[:, 2 * LANES:3 * LANES]
    cum = a_cs + cumcarry_ref[r, 0:1, :]
    cumcarry_ref[r, 0:1, :] = f_lanes(cum[L - 1:L, :])
    cumt_ref[r] = cum.T
    kb = jnp.concatenate(_split3(cum * (-LOG2E)), axis=1)
    kbias_ref[r] = jnp.dot(kb, sel_ref[...], preferred_element_type=F32).astype(BF16)

    a2 = a_cs * LOG2E
    a2t = a2.T

    def col(v, h):
        return jnp.broadcast_to(v[:, h:h + 1], (L, LANES))

    lane = lax.broadcasted_iota(jnp.int32, (L, PAIR), 1)
    lo_half = lane < HEAD_DIM
    heads_per_group = SSD_HEADS // SSD_GROUPS
    pairs_per_group = heads_per_group // 2

    cbs, cms, bms = [], [], []
    for g in range(SSD_GROUPS):
        cm_g = cm[:, g * SSD_STATE:(g + 1) * SSD_STATE]
        bm_g = bm[:, g * SSD_STATE:(g + 1) * SSD_STATE]
        cbs.append(lax.dot_general(cm_g.astype(BF16), bm_g.astype(BF16),
                                   (((1,), (1,)), ((), ())), preferred_element_type=F32))
        cms.append(cm_g)
        bms.append(bm_g)
    yield

    ssq = [None] * SSD_GROUPS
    cd_parts = []
    for p in range(SSD_HEADS // 2):
        g = p // pairs_per_group
        h0, h1 = 2 * p, 2 * p + 1
        sl = slice(p * PAIR, (p + 1) * PAIR)
        xs_p = xs[:, sl]
        acs = (col(a2, h0), col(a2, h1))
        a_pair = jnp.where(lo_half, acs[0], acs[1])
        a_end = a_pair[L - 1:L, :]
        xdt_p = xs_p * jnp.where(lo_half, col(dt, h0), col(dt, h1))
        xdd_ref[r, :, sl] = (xdt_p * jnp.exp2(a_end - a_pair)).astype(BF16)
        cd_parts.append(jnp.exp2(a_end))
        rhs = jnp.concatenate([xdt_p.astype(BF16), state_ref[r, :, sl].astype(BF16)], axis=0)
        ys = []
        for h, ac in zip((h0, h1), acs):
            decay = jnp.exp2(jnp.where(tri, ac - a2t[h:h + 1, :], NEG_BIG))
            lhs = jnp.concatenate([(cbs[g] * decay).astype(BF16),
                                   (cms[g] * jnp.exp2(ac)).astype(BF16)], axis=1)
            ys.append(jnp.dot(lhs, rhs, preferred_element_type=F32))
        y_p = jnp.where(lo_half, ys[0], ys[1]) + dskip_ref[:, sl] * xs_p
        zp = z_ref[r, :, sl].astype(F32)
        yg = y_p * (zp * (1.0 / (1.0 + jnp.exp(-zp))))
        yg_ref[r, :, sl] = yg
        s2 = jnp.sum(yg * yg, axis=-1, keepdims=True)
        ssq[g] = s2 if ssq[g] is None else ssq[g] + s2
        if p % 2 == 1:
            yield

    gw = SSD_WIDTH // SSD_GROUPS
    cd_row = jnp.concatenate(cd_parts, axis=1)
    for g in range(SSD_GROUPS):
        gs = slice(g * gw, (g + 1) * gw)
        scale = lax.rsqrt(ssq[g] * (1.0 / gw) + NORM_EPS)
        y_ref[r, :, gs] = (yg_ref[r, :, gs] * scale * nw_ref[:, gs]).astype(BF16)
        upd = jnp.dot(bms[g].T.astype(BF16), xdd_ref[r, :, gs], preferred_element_type=F32)
        state_ref[r, :, gs] = state_ref[r, :, gs] * cd_row[:, gs] + upd


def _mix_in_body(x_ref, mw_ref, w_ref, g_ref, qw_ref, kw_ref,
                 cw_ref, cb_ref, gbias_ref, alog_ref, dskip_ref, nw_ref, sel_ref, shift_ref,
                 q_ref, k_ref, vt_ref, y_ref, kbias_ref, cumt_ref,
                 state_ref, hist_ref, cumcarry_ref, xdd_ref, yg_ref,
                 xbc_s, z_s, gate_s, ptmp_ref, *, rows_per_step):
    L = SSD_CHUNK
    R = rows_per_step

    @pl.when(pl.program_id(1) == 0)
    def _():
        state_ref[...] = jnp.zeros_like(state_ref)
        hist_ref[...] = jnp.zeros_like(hist_ref)
        cumcarry_ref[...] = jnp.zeros_like(cumcarry_ref)

    x = x_ref[...].reshape(R * L, D_MODEL)
    ms = jnp.mean(x * x, axis=-1, keepdims=True)
    h = (x * lax.rsqrt(ms + NORM_EPS) * mw_ref[...]).astype(BF16)

    def proj(c0, n):
        return jnp.dot(h, w_ref[:, c0:c0 + n], preferred_element_type=F32)

    z_s[...] = proj(_Z0, SSD_WIDTH).astype(BF16).reshape(R, L, SSD_WIDTH)
    xbc_s[...] = proj(_XBC0, SSD_XBC_WIDTH).astype(BF16).reshape(R, L, SSD_XBC_WIDTH)
    gate_s[...] = proj(_GATE0, LANES).reshape(R, L, LANES)

    def normed_heads(base, o_ref, hw_ref):
        ptmp_ref[...] = proj(base, FOX_WIDTH)
        g = g_ref[...]
        for c in range(0, FOX_WIDTH, MXU_DIM):
            t = ptmp_ref[:, c:c + MXU_DIM]
            ss = jnp.dot((t * t).astype(BF16), g, preferred_element_type=F32)
            o_ref[:, :, c:c + MXU_DIM] = (
                t * lax.rsqrt(ss * (1.0 / HEAD_DIM) + NORM_EPS) * hw_ref[...]
            ).astype(BF16).reshape(R, L, MXU_DIM)

    def values_transposed():
        ptmp_ref[...] = proj(_V0, FOX_WIDTH)
        for c in range(0, FOX_WIDTH, MXU_DIM):
            for r in range(R):
                vt_ref[r, c:c + MXU_DIM, :] = ptmp_ref[r * L:(r + 1) * L, c:c + MXU_DIM].T.astype(BF16)

    projections = [functools.partial(normed_heads, _Q0, q_ref, qw_ref),
                   functools.partial(normed_heads, _K0, k_ref, kw_ref),
                   values_transposed]
    chunks = [_ssd_chunk(r, xbc_s, z_s, gate_s, cw_ref, cb_ref, gbias_ref, alog_ref, dskip_ref,
                         nw_ref, sel_ref, shift_ref, y_ref, kbias_ref, cumt_ref,
                         state_ref, hist_ref, cumcarry_ref, xdd_ref, yg_ref) for r in range(R)]
    stage = 0
    while chunks:
        chunks = [c for c in chunks if next(c, _DONE) is not _DONE]
        if 1 <= stage <= len(projections):
            projections[stage - 1]()
        stage += 1


_DONE = object()


_CONV_HIST = 16


def _conv_shift_matrix():
    import numpy as np
    L = SSD_CHUNK
    m = np.zeros(((SSD_CONV - 1) * L, _CONV_HIST + L), np.float32)
    for j in range(1, SSD_CONV):
        for t in range(L):
            m[(j - 1) * L + t, _CONV_HIST + t - j] = 1.0
    return jnp.asarray(m, BF16)


_KBIAS_TERMS = 3


def _kbias_selector():
    import numpy as np
    sel = np.zeros((_KBIAS_TERMS * LANES, FOX_WIDTH), np.float32)
    for h in range(FOX_HEADS):
        base = (h // 2) * PAIR + (HEAD_DIM if h % 2 == 0 else 0)
        for i in range(_KBIAS_TERMS):
            sel[i * LANES + SSD_HEADS + h, base + i] = 1.0
    return jnp.asarray(sel, BF16)


def _mix_in(x, norm_w, w_r, g, qw, kw, conv_w, conv_b, gate_bias, alog_p, dskip_e, ssd_norm_w):
    b, s, _ = x.shape
    L = SSD_CHUNK
    rows = 2 if b % 2 == 0 else 1
    sel = _kbias_selector()
    shift = _conv_shift_matrix()
    blk = lambda n: pl.BlockSpec((rows, L, n), lambda i, j: (i, j, 0))
    consts = (norm_w, w_r, g, qw, kw, conv_w, conv_b, gate_bias, alog_p, dskip_e, ssd_norm_w, sel, shift)
    return pl.pallas_call(
        functools.partial(_mix_in_body, rows_per_step=rows),
        grid=(b // rows, s // L),
        in_specs=[blk(D_MODEL)] + [_const_spec(c.shape) for c in consts],
        out_specs=(blk(FOX_WIDTH), blk(FOX_WIDTH),
                   pl.BlockSpec((rows, FOX_WIDTH, L), lambda i, j: (i, 0, j)),
                   blk(SSD_WIDTH), blk(FOX_WIDTH),
                   pl.BlockSpec((rows, LANES, L), lambda i, j: (i, 0, j))),
        out_shape=(jax.ShapeDtypeStruct((b, s, FOX_WIDTH), BF16),
                   jax.ShapeDtypeStruct((b, s, FOX_WIDTH), BF16),
                   jax.ShapeDtypeStruct((b, FOX_WIDTH, s), BF16),
                   jax.ShapeDtypeStruct((b, s, SSD_WIDTH), BF16),
                   jax.ShapeDtypeStruct((b, s, FOX_WIDTH), BF16),
                   jax.ShapeDtypeStruct((b, LANES, s), F32)),
        scratch_shapes=[
            pltpu.VMEM((rows, SSD_STATE, SSD_WIDTH), F32),
            pltpu.VMEM((2, rows, _CONV_HIST, SSD_XBC_WIDTH), BF16),
            pltpu.VMEM((rows, SUBLANES, LANES), F32),
            pltpu.VMEM((rows, L, SSD_WIDTH), BF16),
            pltpu.VMEM((rows, L, SSD_WIDTH), F32),
            pltpu.VMEM((rows, L, SSD_XBC_WIDTH), BF16),
            pltpu.VMEM((rows, L, SSD_WIDTH), BF16),
            pltpu.VMEM((rows, L, LANES), F32),
            pltpu.VMEM((rows * L, FOX_WIDTH), F32),
        ],
        compiler_params=pltpu.CompilerParams(
            dimension_semantics=("arbitrary", "arbitrary"), vmem_limit_bytes=_VMEM_LIMIT),
        name="mix_in",
    )(x, *consts)


_VAUG_ROWS = HEAD_DIM + 16
_KV_UNROLL = 4


def _fox_body(q_ref, k_ref, vt_ref, kbias_ref, cumt_ref, o_ref,
              kaug_ref, vaug_ref, m_ref, acc_ref, st_a, st_b, *, blk):
    s = q_ref.shape[1]
    nblk = s // blk
    lane = lax.broadcasted_iota(jnp.int32, (blk, PAIR), 1).astype(F32).astype(BF16)
    lo_half = lane < HEAD_DIM

    def build_keys(i, _):
        r0 = pl.multiple_of(i * blk, blk)
        kp = k_ref[0, pl.ds(r0, blk), :]
        kb = kbias_ref[0, pl.ds(r0, blk), :]
        kaug_ref[0, pl.ds(r0, blk), :] = jnp.where(lo_half, kp, kb)
        kaug_ref[1, pl.ds(r0, blk), :] = jnp.where(lo_half, kb, kp)
        for h in range(2):
            vaug_ref[h, 0:HEAD_DIM, pl.ds(r0, blk)] = vt_ref[0, h * HEAD_DIM:(h + 1) * HEAD_DIM,
                                                             pl.ds(r0, blk)]
            vaug_ref[h, HEAD_DIM:_VAUG_ROWS, pl.ds(r0, blk)] = jnp.ones(
                (_VAUG_ROWS - HEAD_DIM, blk), BF16)
        return 0

    lax.fori_loop(0, nblk, build_keys, 0)

    one = jnp.ones((blk, PAIR), BF16)
    zero = jnp.zeros((blk, PAIR), BF16)
    q_ones = (jnp.where(lane < HEAD_DIM + _KBIAS_TERMS, one, zero),
              jnp.where(lane < _KBIAS_TERMS, one, zero))
    key_idx = lax.broadcasted_iota(jnp.int32, (blk, blk), 0)
    qry_idx = lax.broadcasted_iota(jnp.int32, (blk, blk), 1)
    causal = key_idx <= qry_idx

    def queries(qi):
        q = q_ref[0, pl.ds(pl.multiple_of(qi * blk, blk), blk), :]
        return (jnp.where(lo_half, q, q_ones[0]), jnp.where(lo_half, q_ones[1], q))

    def scores(h, j, qa, st_ref):
        k0 = pl.multiple_of(j * blk, blk)
        st_ref[...] = lax.dot_general(kaug_ref[h, pl.ds(k0, blk), :], qa[h],
                                      (((1,), (1,)), ((), ())), preferred_element_type=F32)

    scores(0, 0, queries(0), st_a)

    def q_block(qi, _):
        q0 = pl.multiple_of(qi * blk, blk)
        qa = queries(qi)
        cq = cumt_ref[0, 0, :, pl.ds(q0, blk)] * LOG2E
        m_ref[...] = jnp.full(m_ref.shape, NEG_BIG, F32)
        acc_ref[...] = jnp.zeros_like(acc_ref)

        def softmax_pv(h, k0, st, q_lo, masked):
            nk, nq = st.shape
            qs = slice(q_lo, q_lo + nq)
            if masked:
                st = jnp.where(causal[:nk, :nq], st, NEG_BIG)
            cqh = cq[h:h + 1, qs]
            m_prev = m_ref[h, :, qs]
            m_new = jnp.maximum(m_prev, jnp.max(st, axis=0, keepdims=True) + cqh)
            pt = jnp.exp2(st - (m_new - cqh)).astype(BF16)
            alpha = jnp.exp2(m_prev - m_new)
            acc_ref[h, :, qs] = alpha * acc_ref[h, :, qs] + jnp.dot(
                vaug_ref[h, :, pl.ds(pl.multiple_of(k0, nk), nk)], pt, preferred_element_type=F32)
            m_ref[h, :, qs] = m_new

        def off_diag(j):
            scores(1, j, qa, st_b)
            softmax_pv(0, j * blk, st_a[...], 0, False)
            scores(0, j + 1, qa, st_a)
            softmax_pv(1, j * blk, st_b[...], 0, False)

        def diagonal(h, st_ref):
            half = blk // 2
            softmax_pv(h, q0, st_ref[0:half, :], 0, True)
            softmax_pv(h, q0 + half, st_ref[half:, half:], half, True)

        def run_blocks(j0, n):
            for t in range(n):
                off_diag(j0 + t)

        def unrolled(jj, _):
            run_blocks(_KV_UNROLL * jj, _KV_UNROLL)
            return 0

        lax.fori_loop(0, qi // _KV_UNROLL, unrolled, 0)
        done = qi - lax.rem(qi, _KV_UNROLL)
        n = _KV_UNROLL // 2
        while n >= 1:
            pl.when(lax.bitwise_and(qi, n) != 0)(functools.partial(run_blocks, done, n))
            done = done + lax.bitwise_and(qi, n)
            n //= 2

        scores(1, qi, qa, st_b)
        diagonal(0, st_a)
        qi_next = jnp.minimum(qi + 1, nblk - 1)
        scores(0, 0, queries(qi_next), st_a)
        diagonal(1, st_b)

        out_t = jnp.concatenate(
            [acc_ref[h, 0:HEAD_DIM, :] * (1.0 / acc_ref[h, HEAD_DIM:HEAD_DIM + 1, :])
             for h in range(2)], axis=0)
        o_ref[0, :, pl.ds(q0, blk)] = out_t.astype(BF16)
        return 0

    lax.fori_loop(0, nblk, q_block, 0)


def _fox_attention(q, k, vt, kbias, cumt_pairs, blk):
    b, s, _ = q.shape
    n_pairs = FOX_HEADS // 2
    pair_blk = pl.BlockSpec((1, s, PAIR), lambda i, j: (i, 0, j))
    return pl.pallas_call(
        functools.partial(_fox_body, blk=blk),
        grid=(b, n_pairs),
        in_specs=[pair_blk, pair_blk,
                  pl.BlockSpec((1, PAIR, s), lambda i, j: (i, j, 0)),
                  pair_blk,
                  pl.BlockSpec((1, 1, 2, s), lambda i, j: (i, j, 0, 0))],
        out_specs=pl.BlockSpec((1, PAIR, s), lambda i, j: (i, j, 0)),
        out_shape=jax.ShapeDtypeStruct((b, FOX_WIDTH, s), BF16),
        scratch_shapes=[
            pltpu.VMEM((2, s, PAIR), BF16),
            pltpu.VMEM((2, _VAUG_ROWS, s), BF16),
            pltpu.VMEM((2, 1, blk), F32),
            pltpu.VMEM((2, _VAUG_ROWS, blk), F32),
            pltpu.VMEM((blk, blk), F32),
            pltpu.VMEM((blk, blk), F32),
        ],
        compiler_params=pltpu.CompilerParams(
            dimension_semantics=("arbitrary", "arbitrary"), vmem_limit_bytes=_VMEM_LIMIT),
        name="fox_attention",
    )(q, k, vt, kbias, cumt_pairs)


def _ffn_body(x_ref, ys_ref, yf_ref, wo_ref, nw_ref, wup_ref, cw_ref, cb_ref, wdn_ref,
              o_ref, carry_ref, hg_ref, hv_ref, act_ref, *, tm, ch):
    @pl.when(pl.program_id(1) == 0)
    def _():
        carry_ref[...] = jnp.zeros_like(carry_ref)

    x1 = (x_ref[0]
          + jnp.dot(ys_ref[0], wo_ref[0:SSD_WIDTH, :], preferred_element_type=F32)
          + lax.dot_general(yf_ref[0], wo_ref[SSD_WIDTH:, :], (((0,), (0,)), ((), ())),
                            preferred_element_type=F32))
    o_ref[0] = x1
    ms = jnp.mean(x1 * x1, axis=-1, keepdims=True)
    hf = (x1 * lax.rsqrt(ms + NORM_EPS) * nw_ref[...]).astype(BF16)

    hist = SUBLANES

    def conv_half(h_ref, c0):
        cs = slice(c0, c0 + ch)
        h_ref[0:hist, :] = carry_ref[:, cs]
        h_ref[hist:hist + tm, :] = jnp.dot(hf, wup_ref[:, cs], preferred_element_type=F32)
        carry_ref[:, cs] = h_ref[tm:tm + hist, :]
        out = cb_ref[:, cs] + cw_ref[FFN_CONV - 1:FFN_CONV, cs] * h_ref[hist:hist + tm, :]
        for k in range(FFN_CONV - 1):
            off = hist - (FFN_CONV - 1) + k
            out = out + cw_ref[k:k + 1, cs] * h_ref[off:off + tm, :]
        return out

    for c0 in range(0, D_FF, ch):
        gate = conv_half(hg_ref, c0)
        val = conv_half(hv_ref, D_FF + c0)
        act_ref[:, c0:c0 + ch] = (gate * (1.0 / (1.0 + jnp.exp(-gate))) * val).astype(BF16)

    o_ref[0] = o_ref[0] + jnp.dot(act_ref[...], wdn_ref[...], preferred_element_type=F32)


def _out_ffn(x, y_ssd, y_fox, w_out, norm_w, w_up, conv_w, conv_b, w_down, tm, ch):
    b, s, _ = x.shape
    blk = lambda n: pl.BlockSpec((1, tm, n), lambda i, j: (i, j, 0))
    return pl.pallas_call(
        functools.partial(_ffn_body, tm=tm, ch=ch),
        grid=(b, s // tm),
        in_specs=[blk(D_MODEL), blk(SSD_WIDTH), pl.BlockSpec((1, FOX_WIDTH, tm), lambda i, j: (i, 0, j)),
                  _const_spec(w_out.shape), _const_spec(norm_w.shape), _const_spec(w_up.shape),
                  _const_spec(conv_w.shape), _const_spec(conv_b.shape), _const_spec(w_down.shape)],
        out_specs=blk(D_MODEL),
        out_shape=jax.ShapeDtypeStruct((b, s, D_MODEL), F32),
        scratch_shapes=[
            pltpu.VMEM((SUBLANES, 2 * D_FF), F32),
            pltpu.VMEM((tm + SUBLANES, ch), F32),
            pltpu.VMEM((tm + SUBLANES, ch), F32),
            pltpu.VMEM((tm, D_FF), BF16),
        ],
        compiler_params=pltpu.CompilerParams(
            dimension_semantics=("arbitrary", "arbitrary"), vmem_limit_bytes=_VMEM_LIMIT),
        name="out_ffn",
    )(x, y_ssd, y_fox, w_out, norm_w, w_up, conv_w, conv_b, w_down)


def _pad_lanes(v):
    return jnp.pad(v.astype(F32), (0, LANES - v.shape[0]))[None, :]


def _layer(x, norm_mix_w, w_in, ssd_conv_w, ssd_conv_b, ssd_dt_bias, ssd_a_log, ssd_d, ssd_norm_w,
           fox_f_bias, fox_q_norm_w, fox_k_norm_w, w_out, norm_ffn_w, w_up, ffn_conv_w, ffn_conv_b,
           w_down, *, attn_blk, tm_ffn, ffn_ch):
    b, s, d = x.shape
    z_end = SSD_WIDTH
    xbc_end = z_end + SSD_XBC_WIDTH
    dt_end = xbc_end + SSD_HEADS
    q_end = dt_end + FOX_WIDTH
    k_end = q_end + FOX_WIDTH
    v_end = k_end + FOX_WIDTH

    def pad_cols(w):
        return jnp.pad(w, ((0, 0), (0, LANES - w.shape[1])))

    w_r = jnp.concatenate(
        [w_in[:, :xbc_end], w_in[:, dt_end:v_end],
         pad_cols(jnp.concatenate([w_in[:, xbc_end:dt_end], w_in[:, v_end:]], axis=1))],
        axis=1).astype(BF16)
    heads_per_tile = MXU_DIM // HEAD_DIM
    g = jnp.kron(jnp.eye(heads_per_tile, dtype=F32), jnp.ones((HEAD_DIM, HEAD_DIM), F32)).astype(BF16)
    qw = (jnp.tile(fox_q_norm_w.astype(F32), heads_per_tile) * (HEAD_DIM ** -0.5 * LOG2E))[None, :]
    kw = jnp.tile(fox_k_norm_w.astype(F32), heads_per_tile)[None, :]

    q, k, vt, y_ssd, kbias, cumt = _mix_in(
        x, norm_mix_w[None, :].astype(F32), w_r, g, qw, kw,
        ssd_conv_w.astype(F32), ssd_conv_b[None, :].astype(F32),
        _pad_lanes(jnp.concatenate([ssd_dt_bias, fox_f_bias])), _pad_lanes(ssd_a_log),
        jnp.repeat(ssd_d.astype(F32), HEAD_DIM)[None, :], ssd_norm_w[None, :].astype(F32))

    cumt_pairs = cumt[:, SSD_HEADS:SSD_HEADS + FOX_HEADS, :].reshape(b, FOX_HEADS // 2, 2, s)
    y_fox = _fox_attention(q, k, vt, kbias, cumt_pairs, attn_blk)

    return _out_ffn(x, y_ssd, y_fox, w_out.astype(BF16), norm_ffn_w[None, :].astype(F32),
                    w_up.astype(BF16), ffn_conv_w.astype(F32), ffn_conv_b[None, :].astype(F32),
                    w_down.astype(BF16), tm_ffn, ffn_ch)


def kernel(x, norm_mix_w, w_in, ssd_conv_w, ssd_conv_b, ssd_dt_bias, ssd_a_log, ssd_d, ssd_norm_w,
           fox_f_bias, fox_q_norm_w, fox_k_norm_w, w_out, norm_ffn_w, w_up, ffn_conv_w, ffn_conv_b,
           w_down):
    depth = w_in.shape[0]
    for layer in range(depth):
        x = _layer(x, norm_mix_w[layer], w_in[layer], ssd_conv_w[layer], ssd_conv_b[layer],
                   ssd_dt_bias[layer], ssd_a_log[layer], ssd_d[layer], ssd_norm_w[layer],
                   fox_f_bias[layer], fox_q_norm_w[layer], fox_k_norm_w[layer], w_out[layer],
                   norm_ffn_w[layer], w_up[layer], ffn_conv_w[layer], ffn_conv_b[layer],
                   w_down[layer], attn_blk=512, tm_ffn=512, ffn_ch=256)
    return x
```
